```python
import jax, jax.numpy as jnp
from jax import lax
import numpy as np

D_MODEL = 1024
BATCH = 8
SEQ = 4096
DEPTH = 4
DEC_BATCH = 32
DEC_SEQ = 64
PAST_LEN = 2048

CHUNK = 64
N_MIXERS = 3
N_A = (DEPTH + 2) // 3
N_B = (DEPTH + 1) // 3
N_C = DEPTH // 3
NORM_EPS = 1e-6
NEG_INF = -1e30

A_HEADS = 8
A_HEAD_DIM = D_MODEL // A_HEADS
A_WIDTH = A_HEADS * A_HEAD_DIM
A_BAND_CHUNKS = 9
A_REACH = (A_BAND_CHUNKS - 1) * CHUNK
A_REL_CLIP = 128

B_HEADS = 16
B_KV_HEADS = 4
B_HEAD_DIM = D_MODEL // B_HEADS
B_GROUP = B_HEADS // B_KV_HEADS
B_WIDTH = B_HEADS * B_HEAD_DIM
B_KV_WIDTH = B_KV_HEADS * B_HEAD_DIM
B_WINDOW = 128
B_BAND_CHUNKS = B_WINDOW // CHUNK + 1
B_REACH = B_WINDOW

C_HEADS = 16
C_NOPE = 64
C_ROPE = 32
C_QK = C_NOPE + C_ROPE
C_V = 64
C_Q_LORA = 512
C_KV_LORA = 256
C_WIDTH = C_HEADS * C_V
C_QBLOCK = 128
ROPE_THETA = 10000.0

kernel_name = 'hybrid_stream_encoder_step'


def rmsnorm(x, g):
    xf = x.astype(jnp.float32)
    y = xf * lax.rsqrt(jnp.mean(xf * xf, axis=-1, keepdims=True) + NORM_EPS)
    return (y * g.astype(jnp.float32)).astype(x.dtype)


def rope(x, pos):
    half = x.shape[-1] // 2
    inv = ROPE_THETA ** (-jnp.arange(half, dtype=jnp.float32) / half)
    ang = pos.astype(jnp.float32)[:, None] * inv[None, :]
    shape = (1, x.shape[1]) + (1,) * (x.ndim - 3) + (half,)
    cos = jnp.cos(ang).reshape(shape).astype(x.dtype)
    sin = jnp.sin(ang).reshape(shape).astype(x.dtype)
    x1, x2 = x[..., :half], x[..., half:]
    return jnp.concatenate([x1 * cos - x2 * sin, x1 * sin + x2 * cos], axis=-1)


def alibi_slopes(n):
    return 2.0 ** (-8.0 * jnp.arange(1, n + 1, dtype=jnp.float32) / n)


def attend(q, k, v, scale, bias=None, mask=None, sinks=None):
    s = jnp.einsum('bqhgd,bkhd->bhgqk', q, k).astype(jnp.float32) * scale
    if bias is not None:
        s = s + bias
    if mask is not None:
        s = jnp.where(mask, s, NEG_INF)
    if sinks is None:
        p = jax.nn.softmax(s, axis=-1)
    else:
        sk = sinks.astype(jnp.float32)[None, :, :, None, None]
        m = jnp.maximum(jnp.max(s, axis=-1, keepdims=True), sk)
        e = jnp.exp(s - m)
        p = e / (jnp.sum(e, axis=-1, keepdims=True) + jnp.exp(sk - m))
    return jnp.einsum('bhgqk,bkhe->bqhge', p.astype(v.dtype), v)


def band_attend_prompt(q, k, v, n_band, bias_fn, scale, sinks=None):
    n, s = q.shape[0], q.shape[1]
    pad = (n_band - 1) * CHUNK
    span = n_band * CHUNK
    kp = jnp.pad(k, ((0, 0), (pad, 0), (0, 0), (0, 0)))
    vp = jnp.pad(v, ((0, 0), (pad, 0), (0, 0), (0, 0)))
    kpos = jnp.arange(span)
    bias = bias_fn((pad + jnp.arange(CHUNK))[:, None] - kpos[None, :])

    def one_chunk(c):
        start = c * CHUNK
        qb = lax.dynamic_slice_in_dim(q, start, CHUNK, axis=1)
        kb = lax.dynamic_slice_in_dim(kp, start, span, axis=1)
        vb = lax.dynamic_slice_in_dim(vp, start, span, axis=1)
        valid = (start - pad + kpos) >= 0
        return attend(qb, kb, vb, scale, bias, valid[None, :], sinks)

    out = lax.map(one_chunk, jnp.arange(s // CHUNK))
    return jnp.moveaxis(out, 0, 1).reshape((n, s) + out.shape[3:])


def band_attend_sample(q, k_cache, v_cache, k_new, v_new, bias_fn, scale, sinks=None):
    k = jnp.concatenate([k_cache, k_new], axis=1)
    v = jnp.concatenate([v_cache, v_new], axis=1)
    l, t = k_cache.shape[1], q.shape[1]
    d = (l + jnp.arange(t))[:, None] - jnp.arange(l + t)[None, :]
    return attend(q, k, v, scale, bias_fn(d), None, sinks), k, v


def chunk_causal_attend_prompt(q, k, v, scale):
    n, s = q.shape[0], q.shape[1]
    kchunk = jnp.arange(s) // CHUNK

    def one_block(i):
        start = i * C_QBLOCK
        qb = lax.dynamic_slice_in_dim(q, start, C_QBLOCK, axis=1)
        qchunk = (start + jnp.arange(C_QBLOCK)) // CHUNK
        mask = kchunk[None, :] <= qchunk[:, None]
        return attend(qb, k, v, scale, None, mask)

    out = lax.map(one_block, jnp.arange(s // C_QBLOCK))
    return jnp.moveaxis(out, 0, 1).reshape((n, s) + out.shape[3:])


def gated_out(o, g, w_out):
    return (o.reshape(g.shape) * jax.nn.silu(g)) @ w_out


def mixer_a(xp, xs, ck, cv, norm_g, w_in, q_g, k_g, rel, w_out):
    scale = A_HEAD_DIM ** -0.5

    def bias_fn(d):
        idx = jnp.clip(d, -A_REL_CLIP, A_REL_CLIP) + A_REL_CLIP
        return rel.astype(jnp.float32)[:, idx][:, None]

    def proj(x):
        n, s = x.shape[0], x.shape[1]
        z = rmsnorm(x, norm_g) @ w_in
        q, k, v, g = jnp.split(z, 4, axis=-1)
        q = rmsnorm(q.reshape(n, s, A_HEADS, 1, A_HEAD_DIM), q_g)
        k = rmsnorm(k.reshape(n, s, A_HEADS, A_HEAD_DIM), k_g)
        v = v.reshape(n, s, A_HEADS, A_HEAD_DIM)
        return q, k, v, g

    qp, kp, vp, gp = proj(xp)
    op = band_attend_prompt(qp, kp, vp, A_BAND_CHUNKS, bias_fn, scale)
    qs, ks, vs, gs = proj(xs)
    os_, kall, vall = band_attend_sample(qs, ck, cv, ks, vs, bias_fn, scale)
    return (gated_out(op, gp, w_out), gated_out(os_, gs, w_out),
            kp[:, -A_REACH:], vp[:, -A_REACH:], kall[:, -A_REACH:], vall[:, -A_REACH:])


def mixer_b(xp, xs, ck, cv, norm_g, w_in, q_g, k_g, sinks, w_out):
    scale = B_HEAD_DIM ** -0.5
    slopes = alibi_slopes(B_HEADS)[:, None, None]
    snk = sinks.reshape(B_KV_HEADS, B_GROUP)

    def bias_fn(d):
        return (-slopes * jnp.abs(d).astype(jnp.float32)).reshape(B_KV_HEADS, B_GROUP, d.shape[0], d.shape[1])

    def proj(x):
        n, s = x.shape[0], x.shape[1]
        z = rmsnorm(x, norm_g) @ w_in
        o1 = B_WIDTH
        o2 = o1 + B_KV_WIDTH
        o3 = o2 + B_KV_WIDTH
        q = rmsnorm(z[..., :o1].reshape(n, s, B_KV_HEADS, B_GROUP, B_HEAD_DIM), q_g)
        k = rmsnorm(z[..., o1:o2].reshape(n, s, B_KV_HEADS, B_HEAD_DIM), k_g)
        v = z[..., o2:o3].reshape(n, s, B_KV_HEADS, B_HEAD_DIM)
        return q, k, v, z[..., o3:]

    qp, kp, vp, gp = proj(xp)
    op = band_attend_prompt(qp, kp, vp, B_BAND_CHUNKS, bias_fn, scale, snk)
    qs, ks, vs, gs = proj(xs)
    os_, kall, vall = band_attend_sample(qs, ck, cv, ks, vs, bias_fn, scale, snk)
    return (gated_out(op, gp, w_out), gated_out(os_, gs, w_out),
            kp[:, -B_REACH:], vp[:, -B_REACH:], kall[:, -B_REACH:], vall[:, -B_REACH:])


def mixer_c(xp, xs, ckv_cache, ckr_cache, norm_g, w_in, qa_g, w_qb, kva_g, w_kvb, q_g, k_g, w_out):
    scale = C_QK ** -0.5

    def proj(x, pos):
        n, s = x.shape[0], x.shape[1]
        z = rmsnorm(x, norm_g) @ w_in
        o1 = C_Q_LORA
        o2 = o1 + C_KV_LORA
        o3 = o2 + C_ROPE
        q = (rmsnorm(z[..., :o1], qa_g) @ w_qb).reshape(n, s, C_HEADS, 1, C_QK)
        q = jnp.concatenate([rmsnorm(q[..., :C_NOPE], q_g[:C_NOPE]),
                             rope(rmsnorm(q[..., C_NOPE:], q_g[C_NOPE:]), pos)], axis=-1)
        ckv = rmsnorm(z[..., o1:o2], kva_g)
        kr = rope(rmsnorm(z[..., o2:o3], k_g[C_NOPE:]), pos)
        return q, ckv, kr, z[..., o3:]

    def expand(ckv, kr):
        n, s = ckv.shape[0], ckv.shape[1]
        kv = (ckv @ w_kvb).reshape(n, s, C_HEADS, C_NOPE + C_V)
        k = jnp.concatenate([rmsnorm(kv[..., :C_NOPE], k_g[:C_NOPE]),
                             jnp.broadcast_to(kr[:, :, None, :], (n, s, C_HEADS, C_ROPE))], axis=-1)
        return k, kv[..., C_NOPE:]

    qp, ckv_p, kr_p, gp = proj(xp, jnp.arange(xp.shape[1]))
    kp, vp = expand(ckv_p, kr_p)
    op = chunk_causal_attend_prompt(qp, kp, vp, scale)
    past = ckv_cache.shape[1]
    qs, ckv_s, kr_s, gs = proj(xs, past + jnp.arange(xs.shape[1]))
    ks, vs = expand(jnp.concatenate([ckv_cache, ckv_s], axis=1), jnp.concatenate([ckr_cache, kr_s], axis=1))
    os_ = attend(qs, ks, vs, scale)
    return (gated_out(op, gp, w_out), gated_out(os_, gs, w_out), ckv_p, kr_p, ckv_s, kr_s)


def setup_inputs(seed: int = 0) -> dict:
    key = jax.random.key(seed)
    keys = iter(jax.random.split(key, 40))

    def nrm(shape, scale=1.0):
        return jax.random.normal(next(keys), shape, jnp.float32) * scale

    def gain(shape):
        return 1.0 + 0.02 * nrm(shape)

    la = min(A_REACH, PAST_LEN)
    lb = min(B_REACH, PAST_LEN)
    return {
        'x_prompt': nrm((BATCH, SEQ, D_MODEL)),
        'x_sample': nrm((DEC_BATCH, DEC_SEQ, D_MODEL)),
        'cache_a_k': nrm((N_A, DEC_BATCH, la, A_HEADS, A_HEAD_DIM)),
        'cache_a_v': nrm((N_A, DEC_BATCH, la, A_HEADS, A_HEAD_DIM)),
        'cache_b_k': nrm((N_B, DEC_BATCH, lb, B_KV_HEADS, B_HEAD_DIM)),
        'cache_b_v': nrm((N_B, DEC_BATCH, lb, B_KV_HEADS, B_HEAD_DIM)),
        'cache_c_kv': nrm((N_C, DEC_BATCH, PAST_LEN, C_KV_LORA)),
        'cache_c_kr': nrm((N_C, DEC_BATCH, PAST_LEN, C_ROPE)),
        'a_norm': gain((N_A, D_MODEL)),
        'a_w_in': nrm((N_A, D_MODEL, 4 * A_WIDTH), D_MODEL ** -0.5),
        'a_q_norm': gain((N_A, A_HEAD_DIM)),
        'a_k_norm': gain((N_A, A_HEAD_DIM)),
        'a_rel_bias': nrm((N_A, A_HEADS, 2 * A_REL_CLIP + 1), 0.5),
        'a_w_out': nrm((N_A, A_WIDTH, D_MODEL), A_WIDTH ** -0.5),
        'b_norm': gain((N_B, D_MODEL)),
        'b_w_in': nrm((N_B, D_MODEL, 2 * B_WIDTH + 2 * B_KV_WIDTH), D_MODEL ** -0.5),
        'b_q_norm': gain((N_B, B_HEAD_DIM)),
        'b_k_norm': gain((N_B, B_HEAD_DIM)),
        'b_sinks': nrm((N_B, B_HEADS), 0.5),
        'b_w_out': nrm((N_B, B_WIDTH, D_MODEL), B_WIDTH ** -0.5),
        'c_norm': gain((N_C, D_MODEL)),
        'c_w_in': nrm((N_C, D_MODEL, C_Q_LORA + C_KV_LORA + C_ROPE + C_WIDTH), D_MODEL ** -0.5),
        'c_q_a_norm': gain((N_C, C_Q_LORA)),
        'c_w_qb': nrm((N_C, C_Q_LORA, C_HEADS * C_QK), C_Q_LORA ** -0.5),
        'c_kv_a_norm': gain((N_C, C_KV_LORA)),
        'c_w_kvb': nrm((N_C, C_KV_LORA, C_HEADS * (C_NOPE + C_V)), C_KV_LORA ** -0.5),
        'c_q_norm': gain((N_C, C_QK)),
        'c_k_norm': gain((N_C, C_QK)),
        'c_w_out': nrm((N_C, C_WIDTH, D_MODEL), C_WIDTH ** -0.5),
    }


def reference(x_prompt, x_sample, cache_a_k, cache_a_v, cache_b_k, cache_b_v, cache_c_kv, cache_c_kr,
              a_norm, a_w_in, a_q_norm, a_k_norm, a_rel_bias, a_w_out,
              b_norm, b_w_in, b_q_norm, b_k_norm, b_sinks, b_w_out,
              c_norm, c_w_in, c_q_a_norm, c_w_qb, c_kv_a_norm, c_w_kvb, c_q_norm, c_k_norm, c_w_out):
    xp, xs = x_prompt, x_sample
    ak_p, av_p, ak_s, av_s = [], [], [], []
    bk_p, bv_p, bk_s, bv_s = [], [], [], []
    ckv_p, ckr_p, ckv_s, ckr_s = [], [], [], []
    for i in range(DEPTH):
        j = i // N_MIXERS
        kind = i % N_MIXERS
        if kind == 0:
            dp, ds, s0, s1, s2, s3 = mixer_a(xp, xs, cache_a_k[j], cache_a_v[j], a_norm[j], a_w_in[j],
                                             a_q_norm[j], a_k_norm[j], a_rel_bias[j], a_w_out[j])
            ak_p.append(s0); av_p.append(s1); ak_s.append(s2); av_s.append(s3)
        elif kind == 1:
            dp, ds, s0, s1, s2, s3 = mixer_b(xp, xs, cache_b_k[j], cache_b_v[j], b_norm[j], b_w_in[j],
                                             b_q_norm[j], b_k_norm[j], b_sinks[j], b_w_out[j])
            bk_p.append(s0); bv_p.append(s1); bk_s.append(s2); bv_s.append(s3)
        else:
            dp, ds, s0, s1, s2, s3 = mixer_c(xp, xs, cache_c_kv[j], cache_c_kr[j], c_norm[j], c_w_in[j],
                                             c_q_a_norm[j], c_w_qb[j], c_kv_a_norm[j], c_w_kvb[j],
                                             c_q_norm[j], c_k_norm[j], c_w_out[j])
            ckv_p.append(s0); ckr_p.append(s1); ckv_s.append(s2); ckr_s.append(s3)
        xp = xp + dp
        xs = xs + ds
    return (xp, xs,
            jnp.stack(ak_p), jnp.stack(av_p), jnp.stack(bk_p), jnp.stack(bv_p), jnp.stack(ckv_p), jnp.stack(ckr_p),
            jnp.stack(ak_s), jnp.stack(av_s), jnp.stack(bk_s), jnp.stack(bv_s), jnp.stack(ckv_s), jnp.stack(ckr_s))
```

```python
import functools

import jax
import jax.numpy as jnp
from jax import lax
from jax.experimental import pallas as pl
from jax.experimental.pallas import tpu as pltpu

F32 = jnp.float32
BF16 = jnp.bfloat16

LANES = 128
CHUNK = 64
NORM_EPS = 1e-6
NEG_INF = -1e30
N_MIXERS = 3

A_HEADS = 8
A_HEAD_DIM = 128
A_BAND_CHUNKS = 9
A_REL_CLIP = 128
A_REACH = (A_BAND_CHUNKS - 1) * CHUNK

B_HEADS = 16
B_KV_HEADS = 4
B_HEAD_DIM = 64
B_GROUP = B_HEADS // B_KV_HEADS
B_BAND_CHUNKS = 3
B_REACH = (B_BAND_CHUNKS - 1) * CHUNK

C_HEADS = 16
C_NOPE = 64
C_ROPE = 32
C_QK = C_NOPE + C_ROPE
C_V = 64
C_Q_LORA = 512
C_KV_LORA = 256
ROPE_THETA = 10000.0

ROW_TILE = 512
ATTN_Q_BLOCK = 256
VMEM_LIMIT = 56 * 1024 * 1024


def _params(n_axes):
    return pltpu.CompilerParams(dimension_semantics=("arbitrary",) * n_axes,
                                vmem_limit_bytes=VMEM_LIMIT)


def _const_spec(shape, n_axes):
    zeros = (0,) * len(shape)
    return pl.BlockSpec(shape, lambda *_: zeros)


def _rms(x, width):
    return x * lax.rsqrt(jnp.sum(x * x, axis=-1, keepdims=True) * (1.0 / width) + NORM_EPS)


def _normed_input(x_ref, gn_ref):
    x = x_ref[0]
    return (_rms(x, x.shape[-1]) * gn_ref[...]).astype(BF16)


def _tile(ref_or_val, t):
    return ref_or_val[:, t * LANES:(t + 1) * LANES]


def _proj_a_kernel(x_ref, gn_ref, w_ref, qg_ref, kg_ref,
                   q_ref, k_ref, v_ref, g_ref, kf_ref, vf_ref, *, scale, tail):
    xn = _normed_input(x_ref, gn_ref)
    width = A_HEADS * A_HEAD_DIM
    rows = xn.shape[0]
    zq = jnp.dot(xn, w_ref[:, 0:width], preferred_element_type=F32)
    for h in range(A_HEADS):
        qn = _rms(_tile(zq, h), A_HEAD_DIM) * qg_ref[...]
        q_ref[0, :, h * LANES:(h + 1) * LANES] = (qn * scale).astype(BF16)
    zk = jnp.dot(xn, w_ref[:, width:2 * width], preferred_element_type=F32)
    for h in range(A_HEADS):
        kn = _rms(_tile(zk, h), A_HEAD_DIM) * kg_ref[...]
        k_ref[0, :, h * LANES:(h + 1) * LANES] = kn.astype(BF16)
        kf_ref[0, :, h * LANES:(h + 1) * LANES] = kn[rows - tail:, :]
    zv = jnp.dot(xn, w_ref[:, 2 * width:3 * width], preferred_element_type=F32)
    v_ref[0] = zv.astype(BF16)
    vf_ref[0] = zv[rows - tail:, :]
    g_ref[0] = jnp.dot(xn, w_ref[:, 3 * width:4 * width], preferred_element_type=F32)


def _half_rms(z, lo, width):
    sq = z * z
    s_lo = jnp.sum(jnp.where(lo, sq, 0.0), axis=-1, keepdims=True)
    s_hi = jnp.sum(jnp.where(lo, 0.0, sq), axis=-1, keepdims=True)
    r = jnp.where(lo, lax.rsqrt(s_lo * (1.0 / width) + NORM_EPS),
                  lax.rsqrt(s_hi * (1.0 / width) + NORM_EPS))
    return z * r


def _expand_halves(t, lo, out_ref, u):
    rolled = pltpu.roll(t, B_HEAD_DIM, 1)
    zero = jnp.zeros_like(t)
    tiles = (jnp.where(lo, t, zero), jnp.where(lo, zero, rolled),
             jnp.where(lo, rolled, zero), jnp.where(lo, zero, t))
    for i, val in enumerate(tiles):
        c = 4 * u + i
        out_ref[0, :, c * LANES:(c + 1) * LANES] = val.astype(BF16)


def _proj_b_kernel(x_ref, gn_ref, w_ref, qg_ref, kg_ref,
                   q_ref, k_ref, v_ref, g_ref, kf_ref, vf_ref, *, scale, tail):
    xn = _normed_input(x_ref, gn_ref)
    rows = xn.shape[0]
    qw = B_HEADS * B_HEAD_DIM
    kw = B_KV_HEADS * B_HEAD_DIM
    lo = lax.broadcasted_iota(jnp.int32, (rows, LANES), 1) < B_HEAD_DIM
    zq = jnp.dot(xn, w_ref[:, 0:qw], preferred_element_type=F32)
    for t in range(qw // LANES):
        qn = _half_rms(_tile(zq, t), lo, B_HEAD_DIM) * qg_ref[...]
        q_ref[0, :, t * LANES:(t + 1) * LANES] = (qn * scale).astype(BF16)
    zk = jnp.dot(xn, w_ref[:, qw:qw + kw], preferred_element_type=F32)
    for u in range(kw // LANES):
        kn = _half_rms(_tile(zk, u), lo, B_HEAD_DIM) * kg_ref[...]
        kf_ref[0, :, u * LANES:(u + 1) * LANES] = kn[rows - tail:, :]
        _expand_halves(kn, lo, k_ref, u)
    zv = jnp.dot(xn, w_ref[:, qw + kw:qw + 2 * kw], preferred_element_type=F32)
    vf_ref[0] = zv[rows - tail:, :]
    for u in range(kw // LANES):
        _expand_halves(_tile(zv, u), lo, v_ref, u)
    g_ref[0] = jnp.dot(xn, w_ref[:, qw + 2 * kw:2 * qw + 2 * kw], preferred_element_type=F32)


def _rotate(x, cos_ref, sin_up_ref, sin_dn_ref):
    half = C_ROPE // 2
    return (x * cos_ref[...] + pltpu.roll(x, half, 1) * sin_up_ref[...]
            + pltpu.roll(x, LANES - half, 1) * sin_dn_ref[...])


def _proj_c_kernel(x_ref, gn_ref, w_ref, qag_ref, wqb_ref, kvag_ref, qg_ref, krg_ref,
                   cos_ref, sup_ref, sdn_ref,
                   q_ref, ckv_ref, kr_ref, g_ref, *, scale):
    xn = _normed_input(x_ref, gn_ref)
    rows = xn.shape[0]
    o1 = C_Q_LORA
    o2 = o1 + C_KV_LORA
    o3 = o2 + C_HEADS * C_V
    lane = lax.broadcasted_iota(jnp.int32, (rows, LANES), 1)
    nope = lane < C_NOPE
    za = jnp.dot(xn, w_ref[:, 0:o1], preferred_element_type=F32)
    qa = (_rms(za, C_Q_LORA) * qag_ref[...]).astype(BF16)
    zq = jnp.dot(qa, wqb_ref[...], preferred_element_type=F32)
    for h in range(C_HEADS):
        z = _tile(zq, h)
        sq = z * z
        s_n = jnp.sum(jnp.where(nope, sq, 0.0), axis=-1, keepdims=True)
        s_r = jnp.sum(jnp.where(nope, 0.0, sq), axis=-1, keepdims=True)
        r = jnp.where(nope, lax.rsqrt(s_n * (1.0 / C_NOPE) + NORM_EPS),
                      lax.rsqrt(s_r * (1.0 / C_ROPE) + NORM_EPS))
        qn = _rotate(z * r * qg_ref[...], cos_ref, sup_ref, sdn_ref)
        q_ref[0, :, h * LANES:(h + 1) * LANES] = (qn * scale).astype(BF16)
    zc = jnp.dot(xn, w_ref[:, o1:o2], preferred_element_type=F32)
    ckv_ref[0] = _rms(zc, C_KV_LORA) * kvag_ref[...]
    g_ref[0] = jnp.dot(xn, w_ref[:, o2:o3], preferred_element_type=F32)
    zr = jnp.dot(xn, w_ref[:, o3:o3 + LANES], preferred_element_type=F32)
    krn = _rms(zr, C_ROPE) * krg_ref[...]
    kr_ref[0] = _rotate(krn, cos_ref, sup_ref, sdn_ref)


def _expand_c_kernel(ckv_ref, kr_ref, w_ref, kg_ref, k_ref, v_ref):
    ckv = ckv_ref[0].astype(BF16)
    kr = kr_ref[0]
    width = C_HEADS * LANES
    zk = jnp.dot(ckv, w_ref[:, 0:width], preferred_element_type=F32)
    for h in range(C_HEADS):
        kn = _rms(_tile(zk, h), C_NOPE) * kg_ref[...] + kr
        k_ref[0, :, h * LANES:(h + 1) * LANES] = kn.astype(BF16)
    v_ref[0] = jnp.dot(ckv, w_ref[:, width:2 * width], preferred_element_type=F32).astype(BF16)


def _out_proj_kernel(x_ref, h_ref, w_ref, y_ref):
    y_ref[0] = x_ref[0] + jnp.dot(h_ref[0], w_ref[...], preferred_element_type=F32)


def _silu(g):
    return g / (1.0 + jnp.exp(-g))


def _qk(q, k):
    return lax.dot_general(q, k, (((1,), (1,)), ((), ())), preferred_element_type=F32)


def _window_attn_kernel(*refs, n_pieces, heads, need, has_bias, has_sinks):
    n_tiles = len(heads)
    q_ref = refs[0]
    k_refs = refs[1:1 + n_pieces]
    v_refs = refs[1 + n_pieces:1 + 2 * n_pieces]
    pos = 1 + 2 * n_pieces
    g_ref = refs[pos]
    pos += 1
    bias_ref = None
    if has_bias:
        bias_ref = refs[pos]
        pos += 1
    sink_ref = None
    if has_sinks:
        sink_ref = refs[pos]
        pos += 1
    h_ref = refs[pos]

    step = pl.program_id(1)
    pen = [jnp.where(step >= nd, 0.0, NEG_INF).astype(F32) if nd > 0 else None for nd in need]
    for t in range(n_tiles):
        acc = None
        for sub, (q_t, kv_t) in enumerate(heads[t]):
            head = t * len(heads[t]) + sub
            qt = q_ref[0, :, q_t * LANES:(q_t + 1) * LANES]
            scores = []
            off = 0
            for j in range(n_pieces):
                kj = k_refs[j][0, :, kv_t * LANES:(kv_t + 1) * LANES]
                s = _qk(qt, kj)
                if has_bias:
                    s = s + bias_ref[head, :, off:off + kj.shape[0]]
                if pen[j] is not None:
                    s = s + pen[j]
                scores.append(s)
                off += kj.shape[0]
            m = functools.reduce(jnp.maximum, [jnp.max(s, axis=-1, keepdims=True) for s in scores])
            if has_sinks:
                m = jnp.maximum(m, sink_ref[head])
            probs = [jnp.exp(s - m) for s in scores]
            denom = functools.reduce(jnp.add, [jnp.sum(p, axis=-1, keepdims=True) for p in probs])
            if has_sinks:
                denom = denom + jnp.exp(sink_ref[head] - m)
            o = None
            for j in range(n_pieces):
                vj = v_refs[j][0, :, kv_t * LANES:(kv_t + 1) * LANES]
                pv = jnp.dot(probs[j].astype(BF16), vj, preferred_element_type=F32)
                o = pv if o is None else o + pv
            o = o / denom
            acc = o if acc is None else acc + o
        g = g_ref[0, :, t * LANES:(t + 1) * LANES]
        h_ref[0, :, t * LANES:(t + 1) * LANES] = (acc * _silu(g)).astype(BF16)


def _c_attn_kernel(q_ref, k_ref, v_ref, g_ref, mask_ref, h_ref, *, block):
    step = pl.program_id(1)
    rows = q_ref.shape[1]
    lo = lax.broadcasted_iota(jnp.int32, (rows, LANES), 1) < C_V

    for t in range(C_HEADS // 2):
        q_e = q_ref[0, :, (2 * t) * LANES:(2 * t + 1) * LANES]
        q_o = q_ref[0, :, (2 * t + 1) * LANES:(2 * t + 2) * LANES]

        def update(j, carry, masked, q_e=q_e, q_o=q_o, t=t):
            m_e, l_e, m_o, l_o, acc = carry
            start = pl.multiple_of(j * block, block)
            new = []
            pvs = []
            for q, m_old, l_old, c in ((q_e, m_e, l_e, 2 * t), (q_o, m_o, l_o, 2 * t + 1)):
                kj = k_ref[0, pl.ds(start, block), c * LANES:(c + 1) * LANES]
                vj = v_ref[0, pl.ds(start, block), c * LANES:(c + 1) * LANES]
                s = _qk(q, kj)
                if masked:
                    s = s + mask_ref[...]
                m_new = jnp.maximum(m_old, jnp.max(s, axis=-1, keepdims=True))
                p = jnp.exp(s - m_new)
                alpha = jnp.exp(m_old - m_new)
                l_new = alpha * l_old + jnp.sum(p, axis=-1, keepdims=True)
                pvs.append(jnp.dot(p.astype(BF16), vj, preferred_element_type=F32))
                new.append((m_new, l_new, alpha))
            (m_e2, l_e2, a_e), (m_o2, l_o2, a_o) = new
            acc = acc * jnp.where(lo, a_e, a_o) + pvs[0] + pvs[1]
            return m_e2, l_e2, m_o2, l_o2, acc

        neg = jnp.full((rows, 1), NEG_INF, F32)
        zero = jnp.zeros((rows, 1), F32)
        init = (neg, zero, neg, zero, jnp.zeros((rows, LANES), F32))
        carry = lax.fori_loop(0, step, functools.partial(update, masked=False), init)
        _, l_e, _, l_o, acc = update(step, carry, True)
        g = g_ref[0, :, t * LANES:(t + 1) * LANES]
        o = acc / jnp.where(lo, l_e, l_o)
        h_ref[0, :, t * LANES:(t + 1) * LANES] = (o * _silu(g)).astype(BF16)


def _rows(tm, width):
    return pl.BlockSpec((1, tm, width), lambda b, i: (b, i, 0))


def _cache_out(n, s, tm, width, tail):
    if tail is None:
        return jax.ShapeDtypeStruct((n, s, width), F32), _rows(tm, width), tm
    assert tail <= tm and s % tm == 0
    spec = pl.BlockSpec((1, tail, width), lambda b, i: (b, 0, 0))
    return jax.ShapeDtypeStruct((n, tail, width), F32), spec, tail


def _row_tile(s):
    tm = min(ROW_TILE, s)
    assert s % tm == 0
    return tm


def _proj_ab(kernel, x, gn, w, qg, kg, *, q_width, kv_width, g_width, scale, tail, name):
    n, s, d = x.shape
    tm = _row_tile(s)
    cache_width = (w.shape[1] - q_width - g_width) // 2
    cache_shape, cache_spec, tail_rows = _cache_out(n, s, tm, cache_width, tail)
    return pl.pallas_call(
        functools.partial(kernel, scale=scale, tail=tail_rows),
        grid=(n, s // tm),
        in_specs=[_rows(tm, d), _const_spec(gn.shape, 2), _const_spec(w.shape, 2),
                  _const_spec(qg.shape, 2), _const_spec(kg.shape, 2)],
        out_specs=[_rows(tm, q_width), _rows(tm, kv_width), _rows(tm, kv_width), _rows(tm, g_width),
                   cache_spec, cache_spec],
        out_shape=[jax.ShapeDtypeStruct((n, s, q_width), BF16),
                   jax.ShapeDtypeStruct((n, s, kv_width), BF16),
                   jax.ShapeDtypeStruct((n, s, kv_width), BF16),
                   jax.ShapeDtypeStruct((n, s, g_width), F32), cache_shape, cache_shape],
        compiler_params=_params(2), name=name)(x, gn, w, qg, kg)


def _out_proj(x, h, w):
    n, s, d = x.shape
    tm = _row_tile(s)
    return pl.pallas_call(
        _out_proj_kernel, grid=(n, s // tm),
        in_specs=[_rows(tm, d), _rows(tm, h.shape[-1]), _const_spec(w.shape, 2)],
        out_specs=_rows(tm, d), out_shape=jax.ShapeDtypeStruct((n, s, d), F32),
        compiler_params=_params(2), name="out_proj")(x, h, w)


def _window_attn(q, k_pieces, v_pieces, g, bias, sinks, *, grid, q_map, q_block, heads, need, name):
    width = len(heads) * LANES
    in_specs = [pl.BlockSpec((1, q_block, q.shape[-1]), q_map)]
    args = [q]
    for arr, rows, imap in list(k_pieces) + list(v_pieces):
        in_specs.append(pl.BlockSpec((1, rows, arr.shape[-1]), imap))
        args.append(arr)
    in_specs.append(pl.BlockSpec((1, q_block, width), q_map))
    args.append(g)
    if bias is not None:
        in_specs.append(_const_spec(bias.shape, 2))
        args.append(bias)
    if sinks is not None:
        in_specs.append(pl.BlockSpec(memory_space=pltpu.SMEM))
        args.append(sinks)
    kernel = functools.partial(_window_attn_kernel, n_pieces=len(k_pieces), heads=heads, need=need,
                               has_bias=bias is not None, has_sinks=sinks is not None)
    return pl.pallas_call(
        kernel, grid=grid, in_specs=in_specs,
        out_specs=pl.BlockSpec((1, q_block, width), q_map),
        out_shape=jax.ShapeDtypeStruct(g.shape[:2] + (width,), BF16),
        compiler_params=_params(2), name=name)(*args)


def _band_bias(base, q_rows, k_rows, q_off, band_chunks):
    i = jnp.arange(q_rows)[:, None]
    j = jnp.arange(k_rows)[None, :]
    bias = base(q_off + i - j)
    if band_chunks is None:
        return bias
    lag = (q_off + i) // CHUNK - j // CHUNK
    return jnp.where((lag >= 0) & (lag < band_chunks), bias, NEG_INF)


def _band_prompt_pieces(arr, q_block, reach):
    pieces, need = [], []
    if reach % q_block == 0:
        back = reach // q_block
        for r in range(back, 0, -1):
            pieces.append((arr, q_block, lambda b, i, r=r: (b, jnp.maximum(i - r, 0), 0)))
            need.append(r)
    else:
        assert q_block % reach == 0
        ratio = q_block // reach
        pieces.append((arr, reach, lambda b, i: (b, jnp.maximum(i * ratio - 1, 0), 0)))
        need.append(1)
    pieces.append((arr, q_block, lambda b, i: (b, i, 0)))
    need.append(0)
    return pieces, tuple(need)


def _band_layer(proj, xp, xs, ck, cv, w_out, *, reach, band_chunks, heads, bias_base, sinks,
                expand_cache, name):
    n, s, d = xp.shape
    wo = w_out.astype(BF16)
    qb = min(ATTN_Q_BLOCK, s)
    assert s % qb == 0 and s >= reach

    q, k, v, g, kf, vf = proj(xp, reach)
    kp, need = _band_prompt_pieces(k, qb, reach)
    vp, _ = _band_prompt_pieces(v, qb, reach)
    bias = _band_bias(bias_base, qb, reach + qb, reach, band_chunks)
    h = _window_attn(q, kp, vp, g, bias, sinks, grid=(n, s // qb), q_map=lambda b, i: (b, i, 0),
                     q_block=qb, heads=heads, need=need, name=name + "_attn_prompt")
    yp = _out_proj(xp, h, wo)

    ns, ts, _ = xs.shape
    past = ck.shape[1]
    xs2 = xs.reshape(1, ns * ts, d)
    q, k, v, g, kf_s, vf_s = proj(xs2, None)
    ckf = ck.reshape(ns, past, -1)
    cvf = cv.reshape(ns, past, -1)
    cache_map = lambda b, i: (i, 0, 0)
    new_map = lambda b, i: (0, i, 0)
    bias_s = _band_bias(bias_base, ts, past + ts, past, None)
    h = _window_attn(q, [(expand_cache(ckf), past, cache_map), (k, ts, new_map)],
                     [(expand_cache(cvf), past, cache_map), (v, ts, new_map)], g, bias_s, sinks,
                     grid=(1, ns), q_map=new_map, q_block=ts, heads=heads, need=(0, 0),
                     name=name + "_attn_sample")
    ys = _out_proj(xs2, h, wo).reshape(xs.shape)

    head_shape = ck.shape[2:]
    k_all = jnp.concatenate([ckf, kf_s.reshape(ns, ts, -1)], axis=1)[:, -reach:]
    v_all = jnp.concatenate([cvf, vf_s.reshape(ns, ts, -1)], axis=1)[:, -reach:]
    return (yp, ys, kf.reshape((n, reach) + head_shape), vf.reshape((n, reach) + head_shape),
            k_all.reshape((ns, reach) + head_shape), v_all.reshape((ns, reach) + head_shape))


def _layer_a(xp, xs, ck, cv, norm_g, w_in, q_g, k_g, rel, w_out):
    width = A_HEADS * A_HEAD_DIM
    w = w_in.astype(BF16)
    gn, qg, kg = norm_g.reshape(1, -1), q_g.reshape(1, -1), k_g.reshape(1, -1)

    def proj(x, tail):
        return _proj_ab(_proj_a_kernel, x, gn, w, qg, kg, q_width=width, kv_width=width,
                        g_width=width, scale=A_HEAD_DIM ** -0.5, tail=tail, name="proj_a")

    def bias_base(dist):
        idx = jnp.clip(dist, -A_REL_CLIP, A_REL_CLIP) + A_REL_CLIP
        return rel.astype(F32)[:, idx]

    heads = tuple(((t, t),) for t in range(A_HEADS))
    return _band_layer(proj, xp, xs, ck, cv, w_out, reach=A_REACH, band_chunks=A_BAND_CHUNKS,
                       heads=heads, bias_base=bias_base, sinks=None,
                       expand_cache=lambda c: c.astype(BF16), name="a")


def _layer_b(xp, xs, ck, cv, norm_g, w_in, q_g, k_g, sinks, w_out):
    qw = B_HEADS * B_HEAD_DIM
    kw = B_KV_HEADS * B_HEAD_DIM
    w = w_in.astype(BF16)
    gn = norm_g.reshape(1, -1)
    qg = jnp.tile(q_g.reshape(1, -1), (1, 2))
    kg = jnp.tile(k_g.reshape(1, -1), (1, 2))

    def proj(x, tail):
        return _proj_ab(_proj_b_kernel, x, gn, w, qg, kg, q_width=qw, kv_width=4 * kw, g_width=qw,
                        scale=B_HEAD_DIM ** -0.5, tail=tail, name="proj_b")

    slopes = 2.0 ** (-8.0 * jnp.arange(1, B_HEADS + 1, dtype=F32) / B_HEADS)

    def bias_base(dist):
        return -slopes[:, None, None] * jnp.abs(dist).astype(F32)[None]

    def expand_cache(c):
        n, rows, _ = c.shape
        c4 = c.reshape(n, rows, B_KV_HEADS, 1, B_HEAD_DIM)
        z = jnp.zeros_like(c4)
        lo = jnp.concatenate([c4, z], axis=-1)
        hi = jnp.concatenate([z, c4], axis=-1)
        return jnp.concatenate([lo, hi], axis=3).reshape(n, rows, 4 * kw).astype(BF16)

    heads = tuple(((t, 2 * (t // 2)), (t, 2 * (t // 2) + 1)) for t in range(qw // LANES))
    return _band_layer(proj, xp, xs, ck, cv, w_out, reach=B_REACH, band_chunks=B_BAND_CHUNKS,
                       heads=heads, bias_base=bias_base, sinks=sinks.astype(F32),
                       expand_cache=expand_cache, name="b")


def _rope_tables(pos):
    half = C_ROPE // 2
    inv = ROPE_THETA ** (-jnp.arange(half, dtype=F32) / half)
    ang = pos.astype(F32)[:, None] * inv[None, :]
    cos, sin = jnp.cos(ang), jnp.sin(ang)
    rows = pos.shape[0]
    ones = jnp.ones((rows, C_NOPE), F32)
    zeros = lambda w: jnp.zeros((rows, w), F32)
    pad = LANES - C_QK
    cos_t = jnp.concatenate([ones, cos, cos, zeros(pad)], axis=1)
    sin_up = jnp.concatenate([zeros(C_NOPE + half), sin, zeros(pad)], axis=1)
    sin_dn = jnp.concatenate([zeros(C_NOPE), -sin, zeros(half + pad)], axis=1)
    return cos_t, sin_up, sin_dn


def _proj_c(x, pos, gn, w, qag, wqb, kvag, qg, krg):
    n, s, d = x.shape
    tm = _row_tile(s)
    per_seq = pos.shape[0] // tm
    tables = _rope_tables(pos)
    table_spec = pl.BlockSpec((tm, LANES), lambda b, i: (i % per_seq, 0))
    consts = (gn, w, qag, wqb, kvag, qg, krg)
    return pl.pallas_call(
        functools.partial(_proj_c_kernel, scale=C_QK ** -0.5),
        grid=(n, s // tm),
        in_specs=[_rows(tm, d)] + [_const_spec(c.shape, 2) for c in consts] + [table_spec] * 3,
        out_specs=[_rows(tm, C_HEADS * LANES), _rows(tm, C_KV_LORA), _rows(tm, LANES),
                   _rows(tm, C_HEADS * C_V)],
        out_shape=[jax.ShapeDtypeStruct((n, s, C_HEADS * LANES), BF16),
                   jax.ShapeDtypeStruct((n, s, C_KV_LORA), F32),
                   jax.ShapeDtypeStruct((n, s, LANES), F32),
                   jax.ShapeDtypeStruct((n, s, C_HEADS * C_V), F32)],
        compiler_params=_params(2), name="proj_c")(x, *consts, *tables)


def _expand_c(ckv, kr, w, kg, tk):
    n, s, _ = ckv.shape
    assert s % tk == 0
    width = C_HEADS * LANES
    return pl.pallas_call(
        _expand_c_kernel, grid=(n, s // tk),
        in_specs=[_rows(tk, C_KV_LORA), _rows(tk, LANES), _const_spec(w.shape, 2),
                  _const_spec(kg.shape, 2)],
        out_specs=[_rows(tk, width), _rows(tk, width)],
        out_shape=[jax.ShapeDtypeStruct((n, s, width), BF16)] * 2,
        compiler_params=_params(2), name="expand_c")(ckv, kr, w, kg)


def _c_attn(q, k, v, g):
    n, s, _ = q.shape
    qb = min(ATTN_Q_BLOCK, s)
    assert s % qb == 0 and qb % CHUNK == 0
    row_chunk = jnp.arange(qb)[:, None] // CHUNK
    col_chunk = jnp.arange(qb)[None, :] // CHUNK
    mask = jnp.where(col_chunk <= row_chunk, 0.0, NEG_INF).astype(F32)
    whole = pl.BlockSpec((1, s, k.shape[-1]), lambda b, i: (b, 0, 0), pipeline_mode=pl.Buffered(1))
    return pl.pallas_call(
        functools.partial(_c_attn_kernel, block=qb), grid=(n, s // qb),
        in_specs=[_rows(qb, q.shape[-1]), whole, whole, _rows(qb, g.shape[-1]),
                  _const_spec(mask.shape, 2)],
        out_specs=_rows(qb, g.shape[-1]),
        out_shape=jax.ShapeDtypeStruct(g.shape, BF16),
        compiler_params=_params(2), name="c_attn_prompt")(q, k, v, g, mask)


def _layer_c(xp, xs, ckv_cache, ckr_cache, norm_g, w_in, qa_g, w_qb, kva_g, w_kvb, q_g, k_g, w_out):
    n, s, d = xp.shape
    o1 = C_Q_LORA
    o2 = o1 + C_KV_LORA
    o3 = o2 + C_ROPE
    pad = LANES - C_QK
    kr_cols = jnp.pad(w_in[:, o2:o3], ((0, 0), (C_NOPE, pad)))
    w = jnp.concatenate([w_in[:, :o2], w_in[:, o3:], kr_cols], axis=1).astype(BF16)
    wqb = jnp.pad(w_qb.reshape(C_Q_LORA, C_HEADS, C_QK), ((0, 0), (0, 0), (0, pad)))
    wqb = wqb.reshape(C_Q_LORA, C_HEADS * LANES).astype(BF16)
    kvb = w_kvb.reshape(C_KV_LORA, C_HEADS, C_NOPE + C_V)
    wk = jnp.pad(kvb[..., :C_NOPE], ((0, 0), (0, 0), (0, LANES - C_NOPE)))
    wv = kvb[..., C_NOPE:].reshape(C_KV_LORA, C_HEADS // 2, 2, C_V)
    zv = jnp.zeros_like(wv[:, :, 0])
    wv = jnp.stack([jnp.concatenate([wv[:, :, 0], zv], axis=-1),
                    jnp.concatenate([zv, wv[:, :, 1]], axis=-1)], axis=2)
    wkv = jnp.concatenate([wk.reshape(C_KV_LORA, -1), wv.reshape(C_KV_LORA, -1)], axis=1).astype(BF16)
    wo = w_out.astype(BF16)

    gn, qag, kvag = norm_g.reshape(1, -1), qa_g.reshape(1, -1), kva_g.reshape(1, -1)
    qg = jnp.pad(q_g.reshape(1, -1), ((0, 0), (0, pad)))
    krg = jnp.pad(k_g[C_NOPE:].reshape(1, -1), ((0, 0), (C_NOPE, pad)))
    kg = jnp.pad(k_g[:C_NOPE].reshape(1, -1), ((0, 0), (0, LANES - C_NOPE)))

    q, ckv_p, kr_p, g = _proj_c(xp, jnp.arange(s), gn, w, qag, wqb, kvag, qg, krg)
    k, v = _expand_c(ckv_p, kr_p, wkv, kg, _row_tile(s))
    yp = _out_proj(xp, _c_attn(q, k, v, g), wo)

    ns, ts, _ = xs.shape
    past = ckv_cache.shape[1]
    xs2 = xs.reshape(1, ns * ts, d)
    pos = past + jnp.arange(ts)
    q, ckv_s, kr_s, g = _proj_c(xs2, jnp.tile(pos, ns), gn, w, qag, wqb, kvag, qg, krg)
    ckv_s = ckv_s.reshape(ns, ts, C_KV_LORA)
    kr_s = kr_s.reshape(ns, ts, LANES)
    ckv_all = jnp.concatenate([ckv_cache, ckv_s], axis=1)
    kr_all = jnp.concatenate([jnp.pad(ckr_cache, ((0, 0), (0, 0), (C_NOPE, pad))), kr_s], axis=1)
    total = past + ts
    tk = total // 4 if total % 64 == 0 else total
    k, v = _expand_c(ckv_all, kr_all, wkv, kg, tk)
    whole_map = lambda b, i: (i, 0, 0)
    new_map = lambda b, i: (0, i, 0)
    heads = tuple(((2 * t, 2 * t), (2 * t + 1, 2 * t + 1)) for t in range(C_HEADS // 2))
    h = _window_attn(q, [(k, total, whole_map)], [(v, total, whole_map)], g, None, None,
                     grid=(1, ns), q_map=new_map, q_block=ts, heads=heads, need=(0,),
                     name="c_attn_sample")
    ys = _out_proj(xs2, h, wo).reshape(xs.shape)
    rope = slice(C_NOPE, C_QK)
    return yp, ys, ckv_p, kr_p[..., rope], ckv_s, kr_s[..., rope]


def kernel(x_prompt, x_sample, cache_a_k, cache_a_v, cache_b_k, cache_b_v, cache_c_kv, cache_c_kr,
           a_norm, a_w_in, a_q_norm, a_k_norm, a_rel_bias, a_w_out,
           b_norm, b_w_in, b_q_norm, b_k_norm, b_sinks, b_w_out,
           c_norm, c_w_in, c_q_a_norm, c_w_qb, c_kv_a_norm, c_w_kvb, c_q_norm, c_k_norm, c_w_out):
    depth = a_norm.shape[0] + b_norm.shape[0] + c_norm.shape[0]
    xp, xs = x_prompt, x_sample
    outs = [[] for _ in range(12)]
    for layer in range(depth):
        j, kind = divmod(layer, N_MIXERS)
        if kind == 0:
            res = _layer_a(xp, xs, cache_a_k[j], cache_a_v[j], a_norm[j], a_w_in[j], a_q_norm[j],
                           a_k_norm[j], a_rel_bias[j], a_w_out[j])
        elif kind == 1:
            res = _layer_b(xp, xs, cache_b_k[j], cache_b_v[j], b_norm[j], b_w_in[j], b_q_norm[j],
                           b_k_norm[j], b_sinks[j], b_w_out[j])
        else:
            res = _layer_c(xp, xs, cache_c_kv[j], cache_c_kr[j], c_norm[j], c_w_in[j],
                           c_q_a_norm[j], c_w_qb[j], c_kv_a_norm[j], c_w_kvb[j], c_q_norm[j],
                           c_k_norm[j], c_w_out[j])
        xp, xs = res[0], res[1]
        for slot in range(4):
            outs[4 * kind + slot].append(res[2 + slot])
    stacked = [jnp.stack(o) for o in outs]
    order = [0, 1, 4, 5, 8, 9, 2, 3, 6, 7, 10, 11]
    return (xp, xs) + tuple(stacked[i] for i in order)
```

```python
import functools
import math

import jax
import jax.numpy as jnp
from jax import lax
from jax.experimental import pallas as pl
from jax.experimental.pallas import tpu as pltpu

F32 = jnp.float32
BF16 = jnp.bfloat16

LANES = 128
CHUNK = 64
NORM_EPS = 1e-6
NEG_INF = -1e30
LOG2E = math.log2(math.e)
N_MIXERS = 3

A_HEADS = 8
A_HEAD_DIM = 128
A_BAND_CHUNKS = 9
A_REL_CLIP = 128
A_REACH = (A_BAND_CHUNKS - 1) * CHUNK

B_HEADS = 16
B_KV_HEADS = 4
B_HEAD_DIM = 64
B_BAND_CHUNKS = 3
B_REACH = (B_BAND_CHUNKS - 1) * CHUNK

C_HEADS = 16
C_NOPE = 64
C_ROPE = 32
C_QK = C_NOPE + C_ROPE
C_V = 64
C_Q_LORA = 512
C_KV_LORA = 256
ROPE_THETA = 10000.0

ROW_TILE = 512
ATTN_Q_BLOCK = 256
VMEM_LIMIT = 56 * 1024 * 1024


def _params(n_axes):
    return pltpu.CompilerParams(dimension_semantics=("arbitrary",) * n_axes,
                                vmem_limit_bytes=VMEM_LIMIT)


def _const_spec(shape):
    zeros = (0,) * len(shape)
    return pl.BlockSpec(shape, lambda *_: zeros)


def _rms(x, width):
    return x * lax.rsqrt(jnp.sum(x * x, axis=-1, keepdims=True) * (1.0 / width) + NORM_EPS)


def _normed_input(x_ref, gn_ref):
    x = x_ref[0]
    return (_rms(x, x.shape[-1]) * gn_ref[...]).astype(BF16)


def _tile(val, t):
    return val[:, t * LANES:(t + 1) * LANES]


def _silu(g):
    return g / (1.0 + jnp.exp(-g))


def _qk(q, k):
    return lax.dot_general(q, k, (((1,), (1,)), ((), ())), preferred_element_type=F32)


def _proj_a_kernel(x_ref, gn_ref, w_ref, qg_ref, kg_ref,
                   q_ref, k_ref, v_ref, g_ref, kf_ref, vf_ref, *, scale, tail):
    xn = _normed_input(x_ref, gn_ref)
    width = A_HEADS * A_HEAD_DIM
    rows = xn.shape[0]
    zq = jnp.dot(xn, w_ref[:, 0:width], preferred_element_type=F32)
    for h in range(A_HEADS):
        qn = _rms(_tile(zq, h), A_HEAD_DIM) * qg_ref[...]
        q_ref[0, :, h * LANES:(h + 1) * LANES] = (qn * scale).astype(BF16)
    zk = jnp.dot(xn, w_ref[:, width:2 * width], preferred_element_type=F32)
    for h in range(A_HEADS):
        kn = _rms(_tile(zk, h), A_HEAD_DIM) * kg_ref[...]
        k_ref[0, :, h * LANES:(h + 1) * LANES] = kn.astype(BF16)
        kf_ref[0, :, h * LANES:(h + 1) * LANES] = kn[rows - tail:, :]
    zv = jnp.dot(xn, w_ref[:, 2 * width:3 * width], preferred_element_type=F32)
    v_ref[0] = zv.astype(BF16)
    vf_ref[0] = zv[rows - tail:, :]
    g_ref[0] = jnp.dot(xn, w_ref[:, 3 * width:4 * width], preferred_element_type=F32)


def _half_rms(z, lo, width):
    sq = z * z
    s_lo = jnp.sum(jnp.where(lo, sq, 0.0), axis=-1, keepdims=True)
    s_hi = jnp.sum(jnp.where(lo, 0.0, sq), axis=-1, keepdims=True)
    r = jnp.where(lo, lax.rsqrt(s_lo * (1.0 / width) + NORM_EPS),
                  lax.rsqrt(s_hi * (1.0 / width) + NORM_EPS))
    return z * r


def _expand_halves(t, lo, out_ref, u):
    rolled = pltpu.roll(t, B_HEAD_DIM, 1)
    zero = jnp.zeros_like(t)
    tiles = (jnp.where(lo, t, zero), jnp.where(lo, zero, rolled),
             jnp.where(lo, rolled, zero), jnp.where(lo, zero, t))
    for i, val in enumerate(tiles):
        c = 4 * u + i
        out_ref[0, :, c * LANES:(c + 1) * LANES] = val.astype(BF16)


def _proj_b_kernel(x_ref, gn_ref, w_ref, qg_ref, kg_ref,
                   q_ref, k_ref, v_ref, g_ref, kf_ref, vf_ref, *, scale, tail):
    xn = _normed_input(x_ref, gn_ref)
    rows = xn.shape[0]
    qw = B_HEADS * B_HEAD_DIM
    kw = B_KV_HEADS * B_HEAD_DIM
    lo = lax.broadcasted_iota(jnp.int32, (rows, LANES), 1) < B_HEAD_DIM
    zq = jnp.dot(xn, w_ref[:, 0:qw], preferred_element_type=F32)
    for t in range(qw // LANES):
        qn = _half_rms(_tile(zq, t), lo, B_HEAD_DIM) * qg_ref[...]
        q_ref[0, :, t * LANES:(t + 1) * LANES] = (qn * scale).astype(BF16)
    zk = jnp.dot(xn, w_ref[:, qw:qw + kw], preferred_element_type=F32)
    for u in range(kw // LANES):
        kn = _half_rms(_tile(zk, u), lo, B_HEAD_DIM) * kg_ref[...]
        kf_ref[0, :, u * LANES:(u + 1) * LANES] = kn[rows - tail:, :]
        _expand_halves(kn, lo, k_ref, u)
    zv = jnp.dot(xn, w_ref[:, qw + kw:qw + 2 * kw], preferred_element_type=F32)
    vf_ref[0] = zv[rows - tail:, :]
    for u in range(kw // LANES):
        _expand_halves(_tile(zv, u), lo, v_ref, u)
    g_ref[0] = jnp.dot(xn, w_ref[:, qw + 2 * kw:2 * qw + 2 * kw], preferred_element_type=F32)


def _rotate(x, cos_ref, sin_up_ref, sin_dn_ref):
    half = C_ROPE // 2
    return (x * cos_ref[...] + pltpu.roll(x, half, 1) * sin_up_ref[...]
            + pltpu.roll(x, LANES - half, 1) * sin_dn_ref[...])


def _proj_c_kernel(x_ref, gn_ref, w_ref, qag_ref, wqb_ref, kvag_ref, qg_ref, krg_ref,
                   cos_ref, sup_ref, sdn_ref,
                   q_ref, ckv_ref, kr_ref, g_ref, *, scale):
    xn = _normed_input(x_ref, gn_ref)
    rows = xn.shape[0]
    o1 = C_Q_LORA
    o2 = o1 + C_KV_LORA
    o3 = o2 + C_HEADS * C_V
    lane = lax.broadcasted_iota(jnp.int32, (rows, LANES), 1)
    nope = lane < C_NOPE
    za = jnp.dot(xn, w_ref[:, 0:o1], preferred_element_type=F32)
    qa = (_rms(za, C_Q_LORA) * qag_ref[...]).astype(BF16)
    zq = jnp.dot(qa, wqb_ref[...], preferred_element_type=F32)
    for h in range(C_HEADS):
        z = _tile(zq, h)
        sq = z * z
        s_n = jnp.sum(jnp.where(nope, sq, 0.0), axis=-1, keepdims=True)
        s_r = jnp.sum(jnp.where(nope, 0.0, sq), axis=-1, keepdims=True)
        r = jnp.where(nope, lax.rsqrt(s_n * (1.0 / C_NOPE) + NORM_EPS),
                      lax.rsqrt(s_r * (1.0 / C_ROPE) + NORM_EPS))
        qn = _rotate(z * r * qg_ref[...], cos_ref, sup_ref, sdn_ref)
        q_ref[0, :, h * LANES:(h + 1) * LANES] = (qn * scale).astype(BF16)
    zc = jnp.dot(xn, w_ref[:, o1:o2], preferred_element_type=F32)
    ckv_ref[0] = _rms(zc, C_KV_LORA) * kvag_ref[...]
    g_ref[0] = jnp.dot(xn, w_ref[:, o2:o3], preferred_element_type=F32)
    zr = jnp.dot(xn, w_ref[:, o3:o3 + LANES], preferred_element_type=F32)
    krn = _rms(zr, C_ROPE) * krg_ref[...]
    kr_ref[0] = _rotate(krn, cos_ref, sup_ref, sdn_ref)


def _expand_c_kernel(ckv_ref, kr_ref, w_ref, kg_ref, place_ref, k_ref, v_ref):
    ckv = ckv_ref[0].astype(BF16)
    if kr_ref.shape[-1] == LANES:
        kr = kr_ref[0]
    else:
        kr = jnp.dot(kr_ref[0].astype(BF16), place_ref[...], preferred_element_type=F32)
    width = C_HEADS * LANES
    rows = ckv.shape[0]
    lane = lax.broadcasted_iota(jnp.int32, (rows, LANES), 1)
    zk = jnp.dot(ckv, w_ref[:, 0:width], preferred_element_type=F32)
    for h in range(C_HEADS):
        kn = _rms(_tile(zk, h), C_NOPE) * kg_ref[...] + kr
        k_ref[0, :, h * LANES:(h + 1) * LANES] = kn.astype(BF16)
    zv = jnp.dot(ckv, w_ref[:, width:2 * width], preferred_element_type=F32)
    for h in range(C_HEADS):
        ones_lane = C_V if h % 2 == 0 else 0
        v_ref[0, :, h * LANES:(h + 1) * LANES] = jnp.where(lane == ones_lane, 1.0,
                                                            _tile(zv, h)).astype(BF16)


def _out_proj_kernel(x_ref, h_ref, w_ref, y_ref):
    y_ref[0] = x_ref[0] + jnp.dot(h_ref[0], w_ref[...], preferred_element_type=F32)


def _softmax_parts(scores, sink):
    m = functools.reduce(jnp.maximum, [jnp.max(s, axis=-1, keepdims=True) for s in scores])
    if sink is not None:
        m = jnp.maximum(m, sink)
    probs = [jnp.exp2(s - m) for s in scores]
    denom = functools.reduce(jnp.add, [jnp.sum(p, axis=-1, keepdims=True) for p in probs])
    if sink is not None:
        denom = denom + jnp.exp2(sink - m)
    return probs, denom


def _window_attn_kernel(*refs, n_pieces, heads, need, has_bias, has_sinks):
    q_ref = refs[0]
    k_refs = refs[1:1 + n_pieces]
    v_refs = refs[1 + n_pieces:1 + 2 * n_pieces]
    pos = 1 + 2 * n_pieces
    g_ref = refs[pos]
    pos += 1
    bias_ref = None
    if has_bias:
        bias_ref = refs[pos]
        pos += 1
    sink_ref = None
    if has_sinks:
        sink_ref = refs[pos]
        pos += 1
    h_ref = refs[pos]

    step = pl.program_id(1)
    rows = q_ref.shape[1]
    lo = lax.broadcasted_iota(jnp.int32, (rows, LANES), 1) < LANES // 2
    pen = [jnp.where(step >= nd, 0.0, NEG_INF).astype(F32) if nd > 0 else None for nd in need]
    for t in range(len(heads)):
        acc = None
        for sub, (q_t, kv_t) in enumerate(heads[t]):
            head = t * len(heads[t]) + sub
            qt = q_ref[0, :, q_t * LANES:(q_t + 1) * LANES]
            scores = []
            off = 0
            for j in range(n_pieces):
                kj = k_refs[j][0, :, kv_t * LANES:(kv_t + 1) * LANES]
                s = _qk(qt, kj)
                if has_bias:
                    s = s + bias_ref[head, :, off:off + kj.shape[0]]
                if pen[j] is not None:
                    s = s + pen[j]
                scores.append(s)
                off += kj.shape[0]
            probs, denom = _softmax_parts(scores, sink_ref[head] if has_sinks else None)
            o = None
            for j in range(n_pieces):
                vj = v_refs[j][0, :, kv_t * LANES:(kv_t + 1) * LANES]
                pv = jnp.dot(probs[j].astype(BF16), vj, preferred_element_type=F32)
                o = pv if o is None else o + pv
            o = o / denom
            acc = o if acc is None else jnp.where(lo, acc, o)
        g = g_ref[0, :, t * LANES:(t + 1) * LANES]
        h_ref[0, :, t * LANES:(t + 1) * LANES] = (acc * _silu(g)).astype(BF16)


def _a_sample_kernel(q_ref, kn_ref, vn_ref, knf_ref, vnf_ref, ck_ref, cv_ref, g_ref, bias_ref,
                     *rest):
    h_ref, ok_ref, ov_ref = rest[-3:]
    past = ck_ref.shape[2]
    new = q_ref.shape[1]
    reach = ok_ref.shape[2]
    drop = past + new - reach
    for h in range(A_HEADS):
        cols = slice(h * LANES, (h + 1) * LANES)
        kc = ck_ref[0, 0, :, h, :]
        vc = cv_ref[0, 0, :, h, :]
        qh = q_ref[0, :, cols]
        s_c = _qk(qh, kc.astype(BF16)) + bias_ref[h, :, 0:past]
        s_n = _qk(qh, kn_ref[0, :, cols]) + bias_ref[h, :, past:past + new]
        (p_c, p_n), denom = _softmax_parts([s_c, s_n], None)
        o = (jnp.dot(p_c.astype(BF16), vc.astype(BF16), preferred_element_type=F32)
             + jnp.dot(p_n.astype(BF16), vn_ref[0, :, cols], preferred_element_type=F32)) / denom
        h_ref[0, :, cols] = (o * _silu(g_ref[0, :, cols])).astype(BF16)
        ok_ref[0, 0, 0:past - drop, h, :] = kc[drop:, :]
        ok_ref[0, 0, past - drop:reach, h, :] = knf_ref[0, :, cols]
        ov_ref[0, 0, 0:past - drop, h, :] = vc[drop:, :]
        ov_ref[0, 0, past - drop:reach, h, :] = vnf_ref[0, :, cols]


def _c_attn_kernel(q_ref, k_ref, v_ref, g_ref, mask_ref, h_ref, m_scr, acc_scr, *, kb):
    step = pl.program_id(1)
    rows = q_ref.shape[1]
    lane = lax.broadcasted_iota(jnp.int32, (rows, LANES), 1)
    lo = lane < C_V
    per_q = rows // kb

    m_scr[...] = jnp.full(m_scr.shape, NEG_INF, F32)
    acc_scr[...] = jnp.zeros(acc_scr.shape, F32)

    def block(j, masked):
        start = pl.multiple_of(j * kb, kb)
        for c in range(C_HEADS):
            q = q_ref[0, :, c * LANES:(c + 1) * LANES]
            kj = k_ref[0, pl.ds(start, kb), c * LANES:(c + 1) * LANES]
            vj = v_ref[0, pl.ds(start, kb), c * LANES:(c + 1) * LANES]
            s = _qk(q, kj)
            if masked is not None:
                s = s + mask_ref[masked]
            parts = [s[:, i * LANES:(i + 1) * LANES] for i in range(kb // LANES)]
            m_old = m_scr[c]
            m_new = jnp.maximum(m_old, jnp.max(functools.reduce(jnp.maximum, parts), axis=-1,
                                               keepdims=True))
            m_scr[c] = m_new
            p = jnp.concatenate([jnp.exp2(x - m_new).astype(BF16) for x in parts], axis=1)
            acc_scr[c] = (acc_scr[c] * jnp.exp2(m_old - m_new)
                          + jnp.dot(p, vj, preferred_element_type=F32))

    def body(j, carry):
        block(j, None)
        return carry

    lax.fori_loop(0, step * per_q, body, 0)
    for d in range(per_q):
        block(step * per_q + d, d)

    for t in range(C_HEADS // 2):
        a_e = acc_scr[2 * t]
        a_o = acc_scr[2 * t + 1]
        l_e = jnp.sum(jnp.where(lane == C_V, a_e, 0.0), axis=-1, keepdims=True)
        l_o = jnp.sum(jnp.where(lane == 0, a_o, 0.0), axis=-1, keepdims=True)
        g = g_ref[0, :, t * LANES:(t + 1) * LANES]
        o = jnp.where(lo, a_e / l_e, a_o / l_o)
        h_ref[0, :, t * LANES:(t + 1) * LANES] = (o * _silu(g)).astype(BF16)


def _rows(tm, width):
    return pl.BlockSpec((1, tm, width), lambda b, i: (b, i, 0))


def _cache_out(n, s, tm, width, tail):
    if tail is None:
        return jax.ShapeDtypeStruct((n, s, width), F32), _rows(tm, width), tm
    assert tail <= tm and s % tm == 0
    spec = pl.BlockSpec((1, tail, width), lambda b, i: (b, 0, 0))
    return jax.ShapeDtypeStruct((n, tail, width), F32), spec, tail


def _row_tile(s):
    tm = min(ROW_TILE, s)
    assert s % tm == 0
    return tm


def _proj_ab(kernel, x, gn, w, qg, kg, *, q_width, kv_width, g_width, scale, tail, name):
    n, s, d = x.shape
    tm = _row_tile(s)
    cache_width = (w.shape[1] - q_width - g_width) // 2
    cache_shape, cache_spec, tail_rows = _cache_out(n, s, tm, cache_width, tail)
    return pl.pallas_call(
        functools.partial(kernel, scale=scale, tail=tail_rows),
        grid=(n, s // tm),
        in_specs=[_rows(tm, d), _const_spec(gn.shape), _const_spec(w.shape),
                  _const_spec(qg.shape), _const_spec(kg.shape)],
        out_specs=[_rows(tm, q_width), _rows(tm, kv_width), _rows(tm, kv_width), _rows(tm, g_width),
                   cache_spec, cache_spec],
        out_shape=[jax.ShapeDtypeStruct((n, s, q_width), BF16),
                   jax.ShapeDtypeStruct((n, s, kv_width), BF16),
                   jax.ShapeDtypeStruct((n, s, kv_width), BF16),
                   jax.ShapeDtypeStruct((n, s, g_width), F32), cache_shape, cache_shape],
        compiler_params=_params(2), name=name)(x, gn, w, qg, kg)


def _out_proj(x, h, w):
    n, s, d = x.shape
    tm = _row_tile(s)
    return pl.pallas_call(
        _out_proj_kernel, grid=(n, s // tm),
        in_specs=[_rows(tm, d), _rows(tm, h.shape[-1]), _const_spec(w.shape)],
        out_specs=_rows(tm, d), out_shape=jax.ShapeDtypeStruct((n, s, d), F32),
        compiler_params=_params(2), name="out_proj")(x, h, w)


def _window_attn(q, k_pieces, v_pieces, g, bias, sinks, *, grid, q_map, q_block, heads, need, name):
    width = len(heads) * LANES
    in_specs = [pl.BlockSpec((1, q_block, q.shape[-1]), q_map)]
    args = [q]
    for arr, rows, imap in list(k_pieces) + list(v_pieces):
        in_specs.append(pl.BlockSpec((1, rows, arr.shape[-1]), imap))
        args.append(arr)
    in_specs.append(pl.BlockSpec((1, q_block, width), q_map))
    args.append(g)
    if bias is not None:
        in_specs.append(_const_spec(bias.shape))
        args.append(bias)
    if sinks is not None:
        in_specs.append(pl.BlockSpec(memory_space=pltpu.SMEM))
        args.append(sinks)
    kernel = functools.partial(_window_attn_kernel, n_pieces=len(k_pieces), heads=heads, need=need,
                               has_bias=bias is not None, has_sinks=sinks is not None)
    return pl.pallas_call(
        kernel, grid=grid, in_specs=in_specs,
        out_specs=pl.BlockSpec((1, q_block, width), q_map),
        out_shape=jax.ShapeDtypeStruct(g.shape[:2] + (width,), BF16),
        compiler_params=_params(2), name=name)(*args)


def _band_bias(base, q_rows, k_rows, q_off, band_chunks):
    m = q_rows + k_rows - 1
    vec = base(jnp.arange(m) + (q_off - k_rows + 1)) * LOG2E
    flat = jnp.tile(vec, (1, q_rows + 1))[:, :q_rows * (m + 1)]
    hankel = flat.reshape(-1, q_rows, m + 1)[:, :, :k_rows]
    bias = hankel[:, :, ::-1]
    if band_chunks is None:
        return bias
    i = jnp.arange(q_rows)[:, None]
    j = jnp.arange(k_rows)[None, :]
    lag = (q_off + i) // CHUNK - j // CHUNK
    return jnp.where((lag >= 0) & (lag < band_chunks), bias, NEG_INF)


def _band_prompt_pieces(arr, q_block, reach):
    pieces, need = [], []
    if reach % q_block == 0:
        for r in range(reach // q_block, 0, -1):
            pieces.append((arr, q_block, lambda b, i, r=r: (b, jnp.maximum(i - r, 0), 0)))
            need.append(r)
    else:
        assert q_block % reach == 0
        ratio = q_block // reach
        pieces.append((arr, reach, lambda b, i: (b, jnp.maximum(i * ratio - 1, 0), 0)))
        need.append(1)
    pieces.append((arr, q_block, lambda b, i: (b, i, 0)))
    need.append(0)
    return pieces, tuple(need)


def _band_prompt(proj, xp, wo, *, reach, band_chunks, heads, bias_base, sinks, name):
    n, s, _ = xp.shape
    qb = min(ATTN_Q_BLOCK, s)
    assert s % qb == 0 and s >= reach
    q, k, v, g, kf, vf = proj(xp, reach)
    kp, need = _band_prompt_pieces(k, qb, reach)
    vp, _ = _band_prompt_pieces(v, qb, reach)
    bias = _band_bias(bias_base, qb, reach + qb, reach, band_chunks)
    h = _window_attn(q, kp, vp, g, bias, sinks, grid=(n, s // qb), q_map=lambda b, i: (b, i, 0),
                     q_block=qb, heads=heads, need=need, name=name + "_attn_prompt")
    return _out_proj(xp, h, wo), kf, vf


def _layer_a(xp, xs, cache_k, cache_v, layer, prev_k, prev_v, norm_g, w_in, q_g, k_g, rel, w_out):
    width = A_HEADS * A_HEAD_DIM
    w = w_in.astype(BF16)
    wo = w_out.astype(BF16)
    gn, qg, kg = norm_g.reshape(1, -1), q_g.reshape(1, -1), k_g.reshape(1, -1)

    def proj(x, tail):
        return _proj_ab(_proj_a_kernel, x, gn, w, qg, kg, q_width=width, kv_width=width,
                        g_width=width, scale=A_HEAD_DIM ** -0.5 * LOG2E, tail=tail, name="proj_a")

    def bias_base(dist):
        idx = jnp.clip(dist, -A_REL_CLIP, A_REL_CLIP) + A_REL_CLIP
        return rel.astype(F32)[:, idx]

    heads = tuple(((t, t),) for t in range(A_HEADS))
    yp, kf, vf = _band_prompt(proj, xp, wo, reach=A_REACH, band_chunks=A_BAND_CHUNKS, heads=heads,
                              bias_base=bias_base, sinks=None, name="a")
    n = xp.shape[0]
    head_shape = (A_HEADS, A_HEAD_DIM)
    kf = kf.reshape((n, A_REACH) + head_shape)
    vf = vf.reshape((n, A_REACH) + head_shape)

    n_layers, ns, past = cache_k.shape[:3]
    ts, d = xs.shape[1:]
    xs2 = xs.reshape(1, ns * ts, d)
    q, k, v, g, knf, vnf = proj(xs2, None)
    bias_s = _band_bias(bias_base, ts, past + ts, past, None)
    new = lambda w_: pl.BlockSpec((1, ts, w_), lambda b, i: (0, i, 0))
    cache_block = lambda rows: pl.BlockSpec((1, 1, rows) + head_shape,
                                            lambda b, i: (layer, i, 0, 0, 0))
    out_cache = jax.ShapeDtypeStruct((n_layers, ns, A_REACH) + head_shape, F32)
    prev = [] if prev_k is None else [prev_k, prev_v]
    n_in = 9
    h, k_all, v_all = pl.pallas_call(
        _a_sample_kernel, grid=(1, ns),
        in_specs=[new(width)] * 5 + [cache_block(past)] * 2 + [new(width), _const_spec(bias_s.shape)]
        + [pl.BlockSpec(memory_space=pl.ANY)] * len(prev),
        out_specs=[new(width), cache_block(A_REACH), cache_block(A_REACH)],
        out_shape=[jax.ShapeDtypeStruct((1, ns * ts, width), BF16), out_cache, out_cache],
        input_output_aliases={n_in + i: 1 + i for i in range(len(prev))},
        compiler_params=_params(2), name="a_attn_sample")(
            q, k, v, knf, vnf, cache_k, cache_v, g, bias_s, *prev)
    ys = _out_proj(xs2, h, wo).reshape(xs.shape)
    return yp, ys, kf, vf, k_all, v_all


def _layer_b(xp, xs, ck, cv, norm_g, w_in, q_g, k_g, sinks, w_out):
    qw = B_HEADS * B_HEAD_DIM
    kw = B_KV_HEADS * B_HEAD_DIM
    w = w_in.astype(BF16)
    wo = w_out.astype(BF16)
    gn = norm_g.reshape(1, -1)
    qg = jnp.tile(q_g.reshape(1, -1), (1, 2))
    kg = jnp.tile(k_g.reshape(1, -1), (1, 2))
    sinks2 = sinks.astype(F32) * LOG2E

    def proj(x, tail):
        return _proj_ab(_proj_b_kernel, x, gn, w, qg, kg, q_width=qw, kv_width=4 * kw, g_width=qw,
                        scale=B_HEAD_DIM ** -0.5 * LOG2E, tail=tail, name="proj_b")

    slopes = 2.0 ** (-8.0 * jnp.arange(1, B_HEADS + 1, dtype=F32) / B_HEADS)

    def bias_base(dist):
        return -slopes[:, None] * jnp.abs(dist).astype(F32)[None, :]

    def expand_cache(c):
        n, rows, _ = c.shape
        c4 = c.reshape(n, rows, B_KV_HEADS, 1, B_HEAD_DIM)
        z = jnp.zeros_like(c4)
        lo = jnp.concatenate([c4, z], axis=-1)
        hi = jnp.concatenate([z, c4], axis=-1)
        return jnp.concatenate([lo, hi], axis=3).reshape(n, rows, 4 * kw).astype(BF16)

    heads = tuple(((t, 2 * (t // 2)), (t, 2 * (t // 2) + 1)) for t in range(qw // LANES))
    yp, kf, vf = _band_prompt(proj, xp, wo, reach=B_REACH, band_chunks=B_BAND_CHUNKS, heads=heads,
                              bias_base=bias_base, sinks=sinks2, name="b")
    n = xp.shape[0]
    head_shape = ck.shape[2:]

    ns, ts, d = xs.shape
    past = ck.shape[1]
    xs2 = xs.reshape(1, ns * ts, d)
    q, k, v, g, kf_s, vf_s = proj(xs2, None)
    ckf = ck.reshape(ns, past, -1)
    cvf = cv.reshape(ns, past, -1)
    cache_map = lambda b, i: (i, 0, 0)
    new_map = lambda b, i: (0, i, 0)
    bias_s = _band_bias(bias_base, ts, past + ts, past, None)
    h = _window_attn(q, [(expand_cache(ckf), past, cache_map), (k, ts, new_map)],
                     [(expand_cache(cvf), past, cache_map), (v, ts, new_map)], g, bias_s, sinks2,
                     grid=(1, ns), q_map=new_map, q_block=ts, heads=heads, need=(0, 0),
                     name="b_attn_sample")
    ys = _out_proj(xs2, h, wo).reshape(xs.shape)
    k_all = jnp.concatenate([ckf, kf_s.reshape(ns, ts, -1)], axis=1)[:, -B_REACH:]
    v_all = jnp.concatenate([cvf, vf_s.reshape(ns, ts, -1)], axis=1)[:, -B_REACH:]
    return (yp, ys, kf.reshape((n, B_REACH) + head_shape), vf.reshape((n, B_REACH) + head_shape),
            k_all.reshape((ns, B_REACH) + head_shape), v_all.reshape((ns, B_REACH) + head_shape))


def _rope_tables(pos):
    half = C_ROPE // 2
    inv = ROPE_THETA ** (-jnp.arange(half, dtype=F32) / half)
    ang = pos.astype(F32)[:, None] * inv[None, :]
    cos, sin = jnp.cos(ang), jnp.sin(ang)
    rows = pos.shape[0]
    ones = jnp.ones((rows, C_NOPE), F32)
    zeros = lambda width: jnp.zeros((rows, width), F32)
    pad = LANES - C_QK
    cos_t = jnp.concatenate([ones, cos, cos, zeros(pad)], axis=1)
    sin_up = jnp.concatenate([zeros(C_NOPE + half), sin, zeros(pad)], axis=1)
    sin_dn = jnp.concatenate([zeros(C_NOPE), -sin, zeros(half + pad)], axis=1)
    return cos_t, sin_up, sin_dn


def _proj_c(x, pos, gn, w, qag, wqb, kvag, qg, krg):
    n, s, d = x.shape
    tm = _row_tile(s)
    per_seq = pos.shape[0] // tm
    tables = _rope_tables(pos)
    table_spec = pl.BlockSpec((tm, LANES), lambda b, i: (i % per_seq, 0))
    consts = (gn, w, qag, wqb, kvag, qg, krg)
    return pl.pallas_call(
        functools.partial(_proj_c_kernel, scale=C_QK ** -0.5 * LOG2E),
        grid=(n, s // tm),
        in_specs=[_rows(tm, d)] + [_const_spec(c.shape) for c in consts] + [table_spec] * 3,
        out_specs=[_rows(tm, C_HEADS * LANES), _rows(tm, C_KV_LORA), _rows(tm, LANES),
                   _rows(tm, C_HEADS * C_V)],
        out_shape=[jax.ShapeDtypeStruct((n, s, C_HEADS * LANES), BF16),
                   jax.ShapeDtypeStruct((n, s, C_KV_LORA), F32),
                   jax.ShapeDtypeStruct((n, s, LANES), F32),
                   jax.ShapeDtypeStruct((n, s, C_HEADS * C_V), F32)],
        compiler_params=_params(2), name="proj_c")(x, *consts, *tables)


def _expand_c(ckv, kr, w, kg, first, count):
    s = ckv.shape[1]
    tk = _row_tile(s)
    width = C_HEADS * LANES
    place = jnp.pad(jnp.eye(C_ROPE, dtype=BF16), ((0, 0), (C_NOPE, LANES - C_QK)))
    seq_rows = lambda w_: pl.BlockSpec((1, tk, w_), lambda b, i: (first + b, i, 0))
    return pl.pallas_call(
        _expand_c_kernel, grid=(count, s // tk),
        in_specs=[seq_rows(C_KV_LORA), seq_rows(kr.shape[-1]), _const_spec(w.shape),
                  _const_spec(kg.shape), _const_spec(place.shape)],
        out_specs=[_rows(tk, width), _rows(tk, width)],
        out_shape=[jax.ShapeDtypeStruct((count, s, width), BF16)] * 2,
        compiler_params=_params(2), name="expand_c")(ckv, kr, w, kg, place)


def _c_attn(q, k, v, g):
    n, s, _ = q.shape
    qb = min(ATTN_Q_BLOCK, s)
    kb = qb
    assert s % qb == 0 and qb % kb == 0 and kb % CHUNK == 0
    per_q = qb // kb
    row_chunk = jnp.arange(qb)[None, :, None] // CHUNK
    col_chunk = (jnp.arange(per_q)[:, None, None] * kb + jnp.arange(kb)[None, None, :]) // CHUNK
    mask = jnp.where(col_chunk <= row_chunk, 0.0, NEG_INF).astype(F32)
    whole = pl.BlockSpec((1, s, k.shape[-1]), lambda b, i: (b, 0, 0), pipeline_mode=pl.Buffered(1))
    return pl.pallas_call(
        functools.partial(_c_attn_kernel, kb=kb), grid=(n, s // qb),
        in_specs=[_rows(qb, q.shape[-1]), whole, whole, _rows(qb, g.shape[-1]),
                  _const_spec(mask.shape)],
        out_specs=_rows(qb, g.shape[-1]),
        out_shape=jax.ShapeDtypeStruct(g.shape, BF16),
        scratch_shapes=[pltpu.VMEM((C_HEADS, qb, LANES), F32), pltpu.VMEM((C_HEADS, qb, LANES), F32)],
        compiler_params=_params(2), name="c_attn_prompt")(q, k, v, g, mask)


def _layer_c(xp, xs, cache_kv, cache_kr, layer, norm_g, w_in, qa_g, w_qb, kva_g, w_kvb, q_g, k_g,
             w_out):
    n, s, d = xp.shape
    o1 = C_Q_LORA
    o2 = o1 + C_KV_LORA
    o3 = o2 + C_ROPE
    pad = LANES - C_QK
    kr_cols = jnp.pad(w_in[:, o2:o3], ((0, 0), (C_NOPE, pad)))
    w = jnp.concatenate([w_in[:, :o2], w_in[:, o3:], kr_cols], axis=1).astype(BF16)
    wqb = jnp.pad(w_qb.reshape(C_Q_LORA, C_HEADS, C_QK), ((0, 0), (0, 0), (0, pad)))
    wqb = wqb.reshape(C_Q_LORA, C_HEADS * LANES).astype(BF16)
    kvb = w_kvb.reshape(C_KV_LORA, C_HEADS, C_NOPE + C_V)
    wk = jnp.pad(kvb[..., :C_NOPE], ((0, 0), (0, 0), (0, LANES - C_NOPE)))
    wv = kvb[..., C_NOPE:].reshape(C_KV_LORA, C_HEADS // 2, 2, C_V)
    zv = jnp.zeros_like(wv[:, :, 0])
    wv = jnp.stack([jnp.concatenate([wv[:, :, 0], zv], axis=-1),
                    jnp.concatenate([zv, wv[:, :, 1]], axis=-1)], axis=2)
    wkv = jnp.concatenate([wk.reshape(C_KV_LORA, -1), wv.reshape(C_KV_LORA, -1)], axis=1).astype(BF16)
    wo = w_out.astype(BF16)

    gn, qag, kvag = norm_g.reshape(1, -1), qa_g.reshape(1, -1), kva_g.reshape(1, -1)
    qg = jnp.pad(q_g.reshape(1, -1), ((0, 0), (0, pad)))
    krg = jnp.pad(k_g[C_NOPE:].reshape(1, -1), ((0, 0), (C_NOPE, pad)))
    kg = jnp.pad(k_g[:C_NOPE].reshape(1, -1), ((0, 0), (0, LANES - C_NOPE)))

    q, ckv_p, kr_p, g = _proj_c(xp, jnp.arange(s), gn, w, qag, wqb, kvag, qg, krg)
    k, v = _expand_c(ckv_p, kr_p, wkv, kg, 0, n)
    yp = _out_proj(xp, _c_attn(q, k, v, g), wo)

    n_layers, ns, past = cache_kv.shape[:3]
    ts = xs.shape[1]
    xs2 = xs.reshape(1, ns * ts, d)
    pos = past + jnp.arange(ts)
    q, ckv_s, kr_s, g = _proj_c(xs2, jnp.tile(pos, ns), gn, w, qag, wqb, kvag, qg, krg)
    k_new, v_new = _expand_c(ckv_s, kr_s, wkv, kg, 0, 1)
    k_old, v_old = _expand_c(cache_kv.reshape(n_layers * ns, past, C_KV_LORA),
                             cache_kr.reshape(n_layers * ns, past, C_ROPE), wkv, kg, layer * ns, ns)
    cache_map = lambda b, i: (i, 0, 0)
    new_map = lambda b, i: (0, i, 0)
    heads = tuple(((2 * t, 2 * t), (2 * t + 1, 2 * t + 1)) for t in range(C_HEADS // 2))
    h = _window_attn(q, [(k_old, past, cache_map), (k_new, ts, new_map)],
                     [(v_old, past, cache_map), (v_new, ts, new_map)], g, None, None,
                     grid=(1, ns), q_map=new_map, q_block=ts, heads=heads, need=(0, 0),
                     name="c_attn_sample")
    ys = _out_proj(xs2, h, wo).reshape(xs.shape)
    rope = slice(C_NOPE, C_QK)
    return (yp, ys, ckv_p, kr_p[..., rope], ckv_s.reshape(ns, ts, C_KV_LORA),
            kr_s.reshape(ns, ts, LANES)[..., rope])


def kernel(x_prompt, x_sample, cache_a_k, cache_a_v, cache_b_k, cache_b_v, cache_c_kv, cache_c_kr,
           a_norm, a_w_in, a_q_norm, a_k_norm, a_rel_bias, a_w_out,
           b_norm, b_w_in, b_q_norm, b_k_norm, b_sinks, b_w_out,
           c_norm, c_w_in, c_q_a_norm, c_w_qb, c_kv_a_norm, c_w_kvb, c_q_norm, c_k_norm, c_w_out):
    depth = a_norm.shape[0] + b_norm.shape[0] + c_norm.shape[0]
    xp, xs = x_prompt, x_sample
    outs = [[] for _ in range(12)]
    a_k_all = a_v_all = None
    for layer in range(depth):
        j, kind = divmod(layer, N_MIXERS)
        if kind == 0:
            res = _layer_a(xp, xs, cache_a_k, cache_a_v, j, a_k_all, a_v_all, a_norm[j], a_w_in[j],
                           a_q_norm[j], a_k_norm[j], a_rel_bias[j], a_w_out[j])
            a_k_all, a_v_all = res[4], res[5]
        elif kind == 1:
            res = _layer_b(xp, xs, cache_b_k[j], cache_b_v[j], b_norm[j], b_w_in[j], b_q_norm[j],
                           b_k_norm[j], b_sinks[j], b_w_out[j])
        else:
            res = _layer_c(xp, xs, cache_c_kv, cache_c_kr, j, c_norm[j], c_w_in[j], c_q_a_norm[j],
                           c_w_qb[j], c_kv_a_norm[j], c_w_kvb[j], c_q_norm[j], c_k_norm[j],
                           c_w_out[j])
        xp, xs = res[0], res[1]
        for slot in range(4):
            outs[4 * kind + slot].append(res[2 + slot])
    outs[2], outs[3] = None, None
    stacked = [None if o is None else jnp.stack(o) for o in outs]
    stacked[2], stacked[3] = a_k_all, a_v_all
    order = [0, 1, 4, 5, 8, 9, 2, 3, 6, 7, 10, 11]
    return (xp, xs) + tuple(stacked[i] for i in order)
```

```python
import functools
import math

import jax
import jax.numpy as jnp
from jax import lax
from jax.experimental import pallas as pl
from jax.experimental.pallas import tpu as pltpu

F32 = jnp.float32
BF16 = jnp.bfloat16

LANES = 128
CHUNK = 64
NORM_EPS = 1e-6
NEG_INF = -1e30
LOG2E = math.log2(math.e)
N_MIXERS = 3

A_HEADS = 8
A_HEAD_DIM = 128
A_BAND_CHUNKS = 9
A_REL_CLIP = 128
A_REACH = (A_BAND_CHUNKS - 1) * CHUNK

B_HEADS = 16
B_KV_HEADS = 4
B_HEAD_DIM = 64
B_BAND_CHUNKS = 3
B_REACH = (B_BAND_CHUNKS - 1) * CHUNK

C_HEADS = 16
C_NOPE = 64
C_ROPE = 32
C_QK = C_NOPE + C_ROPE
C_V = 64
C_Q_LORA = 512
C_KV_LORA = 256
ROPE_THETA = 10000.0

ROW_TILE = 512
OUT_ROW_TILE = 1024
ATTN_Q_BLOCK = 256
VMEM_LIMIT = 56 * 1024 * 1024


def _params(n_axes):
    return pltpu.CompilerParams(dimension_semantics=("arbitrary",) * n_axes,
                                vmem_limit_bytes=VMEM_LIMIT)


def _const_spec(shape):
    zeros = (0,) * len(shape)
    return pl.BlockSpec(shape, lambda *_: zeros)


def _rms(x, width):
    return x * lax.rsqrt(jnp.sum(x * x, axis=-1, keepdims=True) * (1.0 / width) + NORM_EPS)


def _normed_input(x_ref, gn_ref):
    x = x_ref[0]
    return (_rms(x, x.shape[-1]) * gn_ref[...]).astype(BF16)


def _tile(val, t):
    return val[:, t * LANES:(t + 1) * LANES]


def _silu(g):
    return g / (1.0 + jnp.exp(-g))


def _qk(q, k):
    return lax.dot_general(q, k, (((1,), (1,)), ((), ())), preferred_element_type=F32)


def _proj_a_kernel(x_ref, gn_ref, w_ref, qg_ref, kg_ref,
                   q_ref, k_ref, v_ref, g_ref, kf_ref, vf_ref, *, scale, tail):
    xn = _normed_input(x_ref, gn_ref)
    width = A_HEADS * A_HEAD_DIM
    rows = xn.shape[0]
    zq = jnp.dot(xn, w_ref[:, 0:width], preferred_element_type=F32)
    for h in range(A_HEADS):
        qn = _rms(_tile(zq, h), A_HEAD_DIM) * qg_ref[...]
        q_ref[0, :, h * LANES:(h + 1) * LANES] = (qn * scale).astype(BF16)
    zk = jnp.dot(xn, w_ref[:, width:2 * width], preferred_element_type=F32)
    for h in range(A_HEADS):
        kn = _rms(_tile(zk, h), A_HEAD_DIM) * kg_ref[...]
        k_ref[0, :, h * LANES:(h + 1) * LANES] = kn.astype(BF16)
        kf_ref[0, :, h * LANES:(h + 1) * LANES] = kn[rows - tail:, :]
    zv = jnp.dot(xn, w_ref[:, 2 * width:3 * width], preferred_element_type=F32)
    v_ref[0] = zv.astype(BF16)
    vf_ref[0] = zv[rows - tail:, :]
    g_ref[0] = jnp.dot(xn, w_ref[:, 3 * width:4 * width], preferred_element_type=F32)


def _half_rms(z, lo, width):
    sq = z * z
    s_lo = jnp.sum(jnp.where(lo, sq, 0.0), axis=-1, keepdims=True)
    s_hi = jnp.sum(jnp.where(lo, 0.0, sq), axis=-1, keepdims=True)
    r = jnp.where(lo, lax.rsqrt(s_lo * (1.0 / width) + NORM_EPS),
                  lax.rsqrt(s_hi * (1.0 / width) + NORM_EPS))
    return z * r


def _expand_halves(t, lo, out_ref, u):
    rolled = pltpu.roll(t, B_HEAD_DIM, 1)
    zero = jnp.zeros_like(t)
    tiles = (jnp.where(lo, t, zero), jnp.where(lo, zero, rolled),
             jnp.where(lo, rolled, zero), jnp.where(lo, zero, t))
    for i, val in enumerate(tiles):
        c = 4 * u + i
        out_ref[0, :, c * LANES:(c + 1) * LANES] = val.astype(BF16)


def _proj_b_kernel(x_ref, gn_ref, w_ref, qg_ref, kg_ref,
                   q_ref, k_ref, v_ref, g_ref, kf_ref, vf_ref, *, scale, tail):
    xn = _normed_input(x_ref, gn_ref)
    rows = xn.shape[0]
    qw = B_HEADS * B_HEAD_DIM
    kw = B_KV_HEADS * B_HEAD_DIM
    lo = lax.broadcasted_iota(jnp.int32, (rows, LANES), 1) < B_HEAD_DIM
    zq = jnp.dot(xn, w_ref[:, 0:qw], preferred_element_type=F32)
    for t in range(qw // LANES):
        qn = _half_rms(_tile(zq, t), lo, B_HEAD_DIM) * qg_ref[...]
        q_ref[0, :, t * LANES:(t + 1) * LANES] = (qn * scale).astype(BF16)
    zk = jnp.dot(xn, w_ref[:, qw:qw + kw], preferred_element_type=F32)
    for u in range(kw // LANES):
        kn = _half_rms(_tile(zk, u), lo, B_HEAD_DIM) * kg_ref[...]
        kf_ref[0, :, u * LANES:(u + 1) * LANES] = kn[rows - tail:, :]
        _expand_halves(kn, lo, k_ref, u)
    zv = jnp.dot(xn, w_ref[:, qw + kw:qw + 2 * kw], preferred_element_type=F32)
    vf_ref[0] = zv[rows - tail:, :]
    for u in range(kw // LANES):
        _expand_halves(_tile(zv, u), lo, v_ref, u)
    g_ref[0] = jnp.dot(xn, w_ref[:, qw + 2 * kw:2 * qw + 2 * kw], preferred_element_type=F32)


def _rotate(x, cos_ref, sin_ref):
    return x * cos_ref[...] + pltpu.roll(x, LANES - C_ROPE // 2, 1) * sin_ref[...]


def _proj_c_kernel(x_ref, gn_ref, w_ref, qag_ref, wqb_ref, kvag_ref, qg_ref, krg_ref,
                   cos_ref, sin_ref, q_ref, ckv_ref, kr_ref, g_ref, *, scale):
    xn = _normed_input(x_ref, gn_ref)
    rows = xn.shape[0]
    o1 = C_Q_LORA
    o2 = o1 + C_KV_LORA
    o3 = o2 + C_HEADS * C_V
    lane = lax.broadcasted_iota(jnp.int32, (rows, LANES), 1)
    nope = lane < C_NOPE
    rope = (lane >= C_NOPE) & (lane < C_QK)
    za = jnp.dot(xn, w_ref[:, 0:o1], preferred_element_type=F32)
    qa = (_rms(za, C_Q_LORA) * qag_ref[...]).astype(BF16)
    zq = jnp.dot(qa, wqb_ref[...], preferred_element_type=F32)
    for h in range(C_HEADS):
        z = _tile(zq, h)
        sq = z * z
        s_n = jnp.sum(jnp.where(nope, sq, 0.0), axis=-1, keepdims=True)
        s_r = jnp.sum(jnp.where(rope, sq, 0.0), axis=-1, keepdims=True)
        r = jnp.where(nope, lax.rsqrt(s_n * (1.0 / C_NOPE) + NORM_EPS),
                      lax.rsqrt(s_r * (1.0 / C_ROPE) + NORM_EPS))
        qn = _rotate(z * r * qg_ref[...], cos_ref, sin_ref)
        q_ref[0, :, h * LANES:(h + 1) * LANES] = (qn * scale).astype(BF16)
    zc = jnp.dot(xn, w_ref[:, o1:o2], preferred_element_type=F32)
    ckv_ref[0] = _rms(zc, C_KV_LORA) * kvag_ref[...]
    g_ref[0] = jnp.dot(xn, w_ref[:, o2:o3], preferred_element_type=F32)
    zr = jnp.dot(xn, w_ref[:, o3:o3 + LANES], preferred_element_type=F32)
    s_k = jnp.sum(jnp.where(rope, zr * zr, 0.0), axis=-1, keepdims=True)
    krn = zr * lax.rsqrt(s_k * (1.0 / C_ROPE) + NORM_EPS) * krg_ref[...]
    kr_ref[0] = _rotate(krn, cos_ref, sin_ref)


def _expand_rows(ckv, kr, w_ref, kg_ref, k_scr, v_scr, rows_at):
    ckv = ckv.astype(BF16)
    width = C_HEADS * LANES
    lane = lax.broadcasted_iota(jnp.int32, (ckv.shape[0], LANES), 1)
    zk = jnp.dot(ckv, w_ref[:, 0:width], preferred_element_type=F32)
    for h in range(C_HEADS):
        kn = _rms(_tile(zk, h), C_NOPE) * kg_ref[...] + kr
        k_scr[rows_at, h * LANES:(h + 1) * LANES] = kn.astype(BF16)
    zv = jnp.dot(ckv, w_ref[:, width:2 * width], preferred_element_type=F32)
    for h in range(C_HEADS):
        ones_lane = C_V if h % 2 == 0 else 0
        v_scr[rows_at, h * LANES:(h + 1) * LANES] = jnp.where(lane == ones_lane, 1.0,
                                                              _tile(zv, h)).astype(BF16)


def _out_proj_kernel(x_ref, h_ref, w_ref, y_ref):
    y_ref[0] = x_ref[0] + jnp.dot(h_ref[0], w_ref[...], preferred_element_type=F32)


def _row_reduce(pieces, combine, reduce):
    tiles, narrow = [], []
    for piece in pieces:
        width = piece.shape[1]
        if width % LANES == 0:
            tiles += [piece[:, i * LANES:(i + 1) * LANES] for i in range(width // LANES)]
        else:
            narrow.append(piece)
    folded = ([functools.reduce(combine, tiles)] if tiles else []) + narrow
    return functools.reduce(combine, [reduce(x, axis=-1, keepdims=True) for x in folded])


def _softmax_parts(scores, sink):
    m = _row_reduce(scores, jnp.maximum, jnp.max)
    if sink is not None:
        m = jnp.maximum(m, sink)
    probs = [jnp.exp2(s - m) for s in scores]
    denom = _row_reduce(probs, jnp.add, jnp.sum)
    if sink is not None:
        denom = denom + jnp.exp2(sink - m)
    return probs, denom


def _window_attn_kernel(*refs, n_pieces, heads, need, has_bias, has_sinks):
    q_ref = refs[0]
    k_refs = refs[1:1 + n_pieces]
    v_refs = refs[1 + n_pieces:1 + 2 * n_pieces]
    pos = 1 + 2 * n_pieces
    g_ref = refs[pos]
    pos += 1
    bias_ref = None
    if has_bias:
        bias_ref = refs[pos]
        pos += 1
    sink_ref = None
    if has_sinks:
        sink_ref = refs[pos]
        pos += 1
    h_ref = refs[pos]

    step = pl.program_id(1)
    rows = q_ref.shape[1]
    lo = lax.broadcasted_iota(jnp.int32, (rows, LANES), 1) < LANES // 2
    pen = [jnp.where(step >= nd, 0.0, NEG_INF).astype(F32) if nd > 0 else None for nd in need]
    for t in range(len(heads)):
        acc = None
        for sub, (q_t, kv_t) in enumerate(heads[t]):
            head = t * len(heads[t]) + sub
            qt = q_ref[0, :, q_t * LANES:(q_t + 1) * LANES]
            scores = []
            off = 0
            for j in range(n_pieces):
                kj = k_refs[j][0, :, kv_t * LANES:(kv_t + 1) * LANES]
                s = _qk(qt, kj)
                if has_bias:
                    s = s + bias_ref[head, :, off:off + kj.shape[0]]
                if pen[j] is not None:
                    s = s + pen[j]
                scores.append(s)
                off += kj.shape[0]
            probs, denom = _softmax_parts(scores, sink_ref[head] if has_sinks else None)
            o = None
            for j in range(n_pieces):
                vj = v_refs[j][0, :, kv_t * LANES:(kv_t + 1) * LANES]
                pv = jnp.dot(probs[j].astype(BF16), vj, preferred_element_type=F32)
                o = pv if o is None else o + pv
            o = o / denom
            acc = o if acc is None else jnp.where(lo, acc, o)
        g = g_ref[0, :, t * LANES:(t + 1) * LANES]
        h_ref[0, :, t * LANES:(t + 1) * LANES] = (acc * _silu(g)).astype(BF16)


def _a_sample_kernel(q_ref, kn_ref, vn_ref, knf_ref, vnf_ref, ck_ref, cv_ref, g_ref, bias_ref,
                     *rest, slot):
    h_ref, ok_ref, ov_ref = rest[-3:]
    past = ck_ref.shape[2]
    new = q_ref.shape[1]
    reach = ok_ref.shape[2]
    drop = past + new - reach
    for other in range(ok_ref.shape[0]):
        if other != slot:
            ok_ref[other] = jnp.zeros(ok_ref.shape[1:], F32)
            ov_ref[other] = jnp.zeros(ov_ref.shape[1:], F32)
    for h in range(A_HEADS):
        cols = slice(h * LANES, (h + 1) * LANES)
        kc = ck_ref[0, 0, :, h, :]
        vc = cv_ref[0, 0, :, h, :]
        qh = q_ref[0, :, cols]
        s_c = _qk(qh, kc) + bias_ref[h, :, 0:past]
        s_n = _qk(qh, kn_ref[0, :, cols]) + bias_ref[h, :, past:past + new]
        (p_c, p_n), denom = _softmax_parts([s_c, s_n], None)
        o = (jnp.dot(p_c.astype(BF16), vc, preferred_element_type=F32)
             + jnp.dot(p_n.astype(BF16), vn_ref[0, :, cols], preferred_element_type=F32)) / denom
        h_ref[0, :, cols] = (o * _silu(g_ref[0, :, cols])).astype(BF16)
        ok_ref[slot, 0, 0:past - drop, h, :] = kc[drop:, :]
        ok_ref[slot, 0, past - drop:reach, h, :] = knf_ref[0, :, cols]
        ov_ref[slot, 0, 0:past - drop, h, :] = vc[drop:, :]
        ov_ref[slot, 0, past - drop:reach, h, :] = vnf_ref[0, :, cols]


def _online_update(c, q, kj, vj, bias, m_scr, acc_scr):
    s = _qk(q, kj)
    if bias is not None:
        s = s + bias
    parts = [s[:, i * LANES:(i + 1) * LANES] for i in range(s.shape[1] // LANES)]
    m_old = m_scr[c]
    m_new = jnp.maximum(m_old, jnp.max(functools.reduce(jnp.maximum, parts), axis=-1,
                                       keepdims=True))
    m_scr[c] = m_new
    p = jnp.concatenate([jnp.exp2(x - m_new).astype(BF16) for x in parts], axis=1)
    acc_scr[c] = (acc_scr[c] * jnp.exp2(m_old - m_new)
                  + jnp.dot(p, vj, preferred_element_type=F32))


def _band_attn_kernel(*refs, n_pieces, heads, need, has_sinks):
    q_ref = refs[0]
    k_refs = refs[1:1 + n_pieces]
    v_refs = refs[1 + n_pieces:1 + 2 * n_pieces]
    g_ref, bias_ref = refs[1 + 2 * n_pieces:3 + 2 * n_pieces]
    pos = 3 + 2 * n_pieces
    sink_ref = None
    if has_sinks:
        sink_ref = refs[pos]
        pos += 1
    h_ref, s_scr, m_scr = refs[pos:pos + 3]

    step = pl.program_id(1)
    rows = q_ref.shape[1]
    lo = lax.broadcasted_iota(jnp.int32, (rows, LANES), 1) < LANES // 2
    flat = [(t, sub, q_t, kv_t) for t in range(len(heads)) for sub, (q_t, kv_t) in enumerate(heads[t])]
    per_tile = len(heads[0])

    offsets = [sum(k_refs[i].shape[1] for i in range(j)) for j in range(n_pieces)]

    for j in range(n_pieces):
        width = k_refs[j].shape[1]
        pen = jnp.where(step >= need[j], 0.0, NEG_INF).astype(F32) if need[j] > 0 else None
        for t, sub, q_t, kv_t in flat:
            c = t * per_tile + sub
            s = _qk(q_ref[0, :, q_t * LANES:(q_t + 1) * LANES],
                    k_refs[j][0, :, kv_t * LANES:(kv_t + 1) * LANES])
            s = s + bias_ref[c, :, offsets[j]:offsets[j] + width]
            if pen is not None:
                s = s + pen
            s_scr[c, :, offsets[j]:offsets[j] + width] = s
            tiles = [s[:, i * LANES:(i + 1) * LANES] for i in range(width // LANES)]
            if j > 0:
                tiles.append(m_scr[c])
            m_scr[c] = functools.reduce(jnp.maximum, tiles)

    slot0 = jnp.minimum(step, 0)
    for t in range(len(heads)):
        out = None
        for sub, (_, kv_t) in enumerate(heads[t]):
            c = t * per_tile + sub
            m = jnp.max(m_scr[c + slot0], axis=-1, keepdims=True)
            if has_sinks:
                m = jnp.maximum(m, sink_ref[c])
            lanes_sum = None
            o = None
            for j in range(n_pieces):
                width = k_refs[j].shape[1]
                p = jnp.exp2(s_scr[c + slot0, :, offsets[j]:offsets[j] + width] - m)
                for i in range(width // LANES):
                    tile = p[:, i * LANES:(i + 1) * LANES]
                    lanes_sum = tile if lanes_sum is None else lanes_sum + tile
                pv = jnp.dot(p.astype(BF16), v_refs[j][0, :, kv_t * LANES:(kv_t + 1) * LANES],
                             preferred_element_type=F32)
                o = pv if o is None else o + pv
            denom = jnp.sum(lanes_sum, axis=-1, keepdims=True)
            if has_sinks:
                denom = denom + jnp.exp2(sink_ref[c] - m)
            o = o / denom
            out = o if out is None else jnp.where(lo, out, o)
        g = g_ref[0, :, t * LANES:(t + 1) * LANES]
        h_ref[0, :, t * LANES:(t + 1) * LANES] = (out * _silu(g)).astype(BF16)


def _c_attn_kernel(q_ref, ckv_ref, kr_ref, w_ref, kg_ref, g_ref, mask_ref, h_ref,
                   k_scr, v_scr, m_scr, acc_scr, *, kb, expand_tile):
    step = pl.program_id(1)
    rows = q_ref.shape[1]
    lane = lax.broadcasted_iota(jnp.int32, (rows, LANES), 1)
    lo = lane < C_V
    per_q = rows // kb

    @pl.when(step == 0)
    def _expand_sequence():
        def expand(r, carry):
            at = pl.ds(pl.multiple_of(r * expand_tile, expand_tile), expand_tile)
            _expand_rows(ckv_ref[0, at, :], kr_ref[0, at, :], w_ref, kg_ref, k_scr, v_scr, at)
            return carry
        lax.fori_loop(0, ckv_ref.shape[1] // expand_tile, expand, 0)

    m_scr[...] = jnp.full(m_scr.shape, NEG_INF, F32)
    acc_scr[...] = jnp.zeros(acc_scr.shape, F32)

    def block(j, masked):
        start = pl.multiple_of(j * kb, kb)
        for c in range(C_HEADS):
            cols = slice(c * LANES, (c + 1) * LANES)
            _online_update(c, q_ref[0, :, cols], k_scr[pl.ds(start, kb), cols],
                           v_scr[pl.ds(start, kb), cols],
                           None if masked is None else mask_ref[masked], m_scr, acc_scr)

    def body(j, carry):
        block(j, None)
        return carry

    lax.fori_loop(0, step * per_q, body, 0)
    for d in range(per_q):
        block(step * per_q + d, d)

    for t in range(C_HEADS // 2):
        a_e = acc_scr[2 * t]
        a_o = acc_scr[2 * t + 1]
        l_e = jnp.sum(jnp.where(lane == C_V, a_e, 0.0), axis=-1, keepdims=True)
        l_o = jnp.sum(jnp.where(lane == 0, a_o, 0.0), axis=-1, keepdims=True)
        g = g_ref[0, :, t * LANES:(t + 1) * LANES]
        o = jnp.where(lo, a_e / l_e, a_o / l_o)
        h_ref[0, :, t * LANES:(t + 1) * LANES] = (o * _silu(g)).astype(BF16)


def _c_sample_kernel(q_ref, ckv_ref, ckr_ref, nkv_ref, nkr_ref, w_ref, kg_ref, place_ref, g_ref,
                     h_ref, k_scr, v_scr, *, expand_tile):
    past = ckv_ref.shape[1]
    new = q_ref.shape[1]
    lo = lax.broadcasted_iota(jnp.int32, (new, LANES), 1) < C_V

    def expand(r, carry):
        at = pl.ds(pl.multiple_of(r * expand_tile, expand_tile), expand_tile)
        kr = jnp.dot(ckr_ref[0, at, :].astype(BF16), place_ref[...], preferred_element_type=F32)
        _expand_rows(ckv_ref[0, at, :], kr, w_ref, kg_ref, k_scr, v_scr, at)
        return carry
    lax.fori_loop(0, past // expand_tile, expand, 0)
    _expand_rows(nkv_ref[0], nkr_ref[0], w_ref, kg_ref, k_scr, v_scr, pl.ds(past, new))

    for t in range(C_HEADS // 2):
        acc = None
        for c in (2 * t, 2 * t + 1):
            cols = slice(c * LANES, (c + 1) * LANES)
            q = q_ref[0, :, cols]
            s_c = _qk(q, k_scr[0:past, cols])
            s_n = _qk(q, k_scr[past:past + new, cols])
            (p_c, p_n), denom = _softmax_parts([s_c, s_n], None)
            o = (jnp.dot(p_c.astype(BF16), v_scr[0:past, cols], preferred_element_type=F32)
                 + jnp.dot(p_n.astype(BF16), v_scr[past:past + new, cols],
                           preferred_element_type=F32)) / denom
            acc = o if acc is None else jnp.where(lo, acc, o)
        g = g_ref[0, :, t * LANES:(t + 1) * LANES]
        h_ref[0, :, t * LANES:(t + 1) * LANES] = (acc * _silu(g)).astype(BF16)


def _rows(tm, width):
    return pl.BlockSpec((1, tm, width), lambda b, i: (b, i, 0))


def _cache_out(n, s, tm, width, tail):
    if tail is None:
        return jax.ShapeDtypeStruct((n, s, width), F32), _rows(tm, width), tm
    assert tail <= tm and s % tm == 0
    spec = pl.BlockSpec((1, tail, width), lambda b, i: (b, 0, 0))
    return jax.ShapeDtypeStruct((n, tail, width), F32), spec, tail


def _row_tile(s):
    tm = min(ROW_TILE, s)
    assert s % tm == 0
    return tm


def _proj_ab(kernel, x, gn, w, qg, kg, *, q_width, kv_width, g_width, scale, tail, name):
    n, s, d = x.shape
    tm = _row_tile(s)
    cache_width = (w.shape[1] - q_width - g_width) // 2
    cache_shape, cache_spec, tail_rows = _cache_out(n, s, tm, cache_width, tail)
    return pl.pallas_call(
        functools.partial(kernel, scale=scale, tail=tail_rows),
        grid=(n, s // tm),
        in_specs=[_rows(tm, d), _const_spec(gn.shape), _const_spec(w.shape),
                  _const_spec(qg.shape), _const_spec(kg.shape)],
        out_specs=[_rows(tm, q_width), _rows(tm, kv_width), _rows(tm, kv_width), _rows(tm, g_width),
                   cache_spec, cache_spec],
        out_shape=[jax.ShapeDtypeStruct((n, s, q_width), BF16),
                   jax.ShapeDtypeStruct((n, s, kv_width), BF16),
                   jax.ShapeDtypeStruct((n, s, kv_width), BF16),
                   jax.ShapeDtypeStruct((n, s, g_width), F32), cache_shape, cache_shape],
        compiler_params=_params(2), name=name)(x, gn, w, qg, kg)


def _out_proj(x, h, w):
    n, s, d = x.shape
    tm = math.gcd(s, OUT_ROW_TILE)
    return pl.pallas_call(
        _out_proj_kernel, grid=(n, s // tm),
        in_specs=[_rows(tm, d), _rows(tm, h.shape[-1]), _const_spec(w.shape)],
        out_specs=_rows(tm, d), out_shape=jax.ShapeDtypeStruct((n, s, d), F32),
        compiler_params=_params(2), name="out_proj")(x, h, w)


def _window_attn(q, k_pieces, v_pieces, g, bias, sinks, *, grid, q_map, q_block, heads, need, name):
    width = len(heads) * LANES
    in_specs = [pl.BlockSpec((1, q_block, q.shape[-1]), q_map)]
    args = [q]
    for arr, rows, imap in list(k_pieces) + list(v_pieces):
        in_specs.append(pl.BlockSpec((1, rows, arr.shape[-1]), imap))
        args.append(arr)
    in_specs.append(pl.BlockSpec((1, q_block, width), q_map))
    args.append(g)
    if bias is not None:
        in_specs.append(_const_spec(bias.shape))
        args.append(bias)
    if sinks is not None:
        in_specs.append(pl.BlockSpec(memory_space=pltpu.SMEM))
        args.append(sinks)
    kernel = functools.partial(_window_attn_kernel, n_pieces=len(k_pieces), heads=heads, need=need,
                               has_bias=bias is not None, has_sinks=sinks is not None)
    return pl.pallas_call(
        kernel, grid=grid, in_specs=in_specs,
        out_specs=pl.BlockSpec((1, q_block, width), q_map),
        out_shape=jax.ShapeDtypeStruct(g.shape[:2] + (width,), BF16),
        compiler_params=_params(2), name=name)(*args)


def _band_bias(base, q_rows, k_rows, q_off, band_chunks):
    m = q_rows + k_rows - 1
    vec = base(q_off + (q_rows - 1) - jnp.arange(m)) * LOG2E
    flat = jnp.tile(vec, (1, q_rows))[:, q_rows - 1:q_rows - 1 + q_rows * (m - 1)]
    bias = flat.reshape(-1, q_rows, m - 1)[:, :, :k_rows]
    if band_chunks is None:
        return bias
    i = jnp.arange(q_rows)[:, None]
    j = jnp.arange(k_rows)[None, :]
    lag = (q_off + i) // CHUNK - j // CHUNK
    return jnp.where((lag >= 0) & (lag < band_chunks), bias, NEG_INF)


def _band_prompt_pieces(arr, q_block, reach):
    pieces, need = [], []
    if reach % q_block == 0:
        for r in range(reach // q_block, 0, -1):
            pieces.append((arr, q_block, lambda b, i, r=r: (b, jnp.maximum(i - r, 0), 0)))
            need.append(r)
    else:
        assert q_block % reach == 0
        ratio = q_block // reach
        pieces.append((arr, reach, lambda b, i: (b, jnp.maximum(i * ratio - 1, 0), 0)))
        need.append(1)
    pieces.append((arr, q_block, lambda b, i: (b, i, 0)))
    need.append(0)
    return pieces, tuple(need)


def _band_prompt(proj, xp, wo, *, reach, band_chunks, heads, bias_base, sinks, name):
    n, s, _ = xp.shape
    qb = min(ATTN_Q_BLOCK, s)
    assert s % qb == 0 and s >= reach
    q, k, v, g, kf, vf = proj(xp, reach)
    kp, need = _band_prompt_pieces(k, qb, reach)
    vp, _ = _band_prompt_pieces(v, qb, reach)
    bias = _band_bias(bias_base, qb, reach + qb, reach, band_chunks)
    width = len(heads) * LANES
    n_slots = sum(len(hs) for hs in heads)
    pieces = kp + vp
    args = [q] + [arr for arr, _, _ in pieces] + [g, bias]
    in_specs = ([_rows(qb, q.shape[-1])]
                + [pl.BlockSpec((1, rows, arr.shape[-1]), imap) for arr, rows, imap in pieces]
                + [_rows(qb, width), _single(bias.shape, lambda b, i: (0, 0, 0))])
    if sinks is not None:
        in_specs.append(pl.BlockSpec(memory_space=pltpu.SMEM))
        args.append(sinks)
    scratch = [pltpu.VMEM((n_slots, qb, reach + qb), F32), pltpu.VMEM((n_slots, qb, LANES), F32)]
    h = pl.pallas_call(
        functools.partial(_band_attn_kernel, n_pieces=len(kp), heads=heads, need=need,
                          has_sinks=sinks is not None),
        grid=(n, s // qb), in_specs=in_specs, out_specs=_rows(qb, width),
        out_shape=jax.ShapeDtypeStruct((n, s, width), BF16),
        scratch_shapes=scratch,
        compiler_params=_params(2), name=name + "_attn_prompt")(*args)
    return _out_proj(xp, h, wo), kf, vf


def _layer_a(xp, xs, cache_k, cache_v, layer, prev_k, prev_v, norm_g, w_in, q_g, k_g, rel, w_out):
    width = A_HEADS * A_HEAD_DIM
    w = w_in.astype(BF16)
    wo = w_out.astype(BF16)
    gn, qg, kg = norm_g.reshape(1, -1), q_g.reshape(1, -1), k_g.reshape(1, -1)

    def proj(x, tail):
        return _proj_ab(_proj_a_kernel, x, gn, w, qg, kg, q_width=width, kv_width=width,
                        g_width=width, scale=A_HEAD_DIM ** -0.5 * LOG2E, tail=tail, name="proj_a")

    def bias_base(dist):
        idx = jnp.clip(dist, -A_REL_CLIP, A_REL_CLIP) + A_REL_CLIP
        return rel.astype(F32)[:, idx]

    heads = tuple(((t, t),) for t in range(A_HEADS))
    yp, kf, vf = _band_prompt(proj, xp, wo, reach=A_REACH, band_chunks=A_BAND_CHUNKS, heads=heads,
                              bias_base=bias_base, sinks=None, name="a")
    n = xp.shape[0]
    head_shape = (A_HEADS, A_HEAD_DIM)
    kf = kf.reshape((n, A_REACH) + head_shape)
    vf = vf.reshape((n, A_REACH) + head_shape)

    n_layers, ns, past = cache_k.shape[:3]
    ts, d = xs.shape[1:]
    xs2 = xs.reshape(1, ns * ts, d)
    q, k, v, g, knf, vnf = proj(xs2, None)
    bias_s = _band_bias(bias_base, ts, past + ts, past, None)
    new = lambda w_: pl.BlockSpec((1, ts, w_), lambda b, i: (0, i, 0))
    cache_block = lambda rows: pl.BlockSpec((1, 1, rows) + head_shape,
                                            lambda b, i: (layer, i, 0, 0, 0))
    out_cache = jax.ShapeDtypeStruct((n_layers, ns, A_REACH) + head_shape, F32)
    prev = [] if prev_k is None else [prev_k, prev_v]
    if prev:
        out_block, slot = cache_block(A_REACH), 0
    else:
        out_block = pl.BlockSpec((n_layers, 1, A_REACH) + head_shape, lambda b, i: (0, i, 0, 0, 0))
        slot = layer
    n_in = 9
    h, k_all, v_all = pl.pallas_call(
        functools.partial(_a_sample_kernel, slot=slot), grid=(1, ns),
        in_specs=[new(width)] * 5 + [cache_block(past)] * 2 + [new(width), _const_spec(bias_s.shape)]
        + [pl.BlockSpec(memory_space=pl.ANY)] * len(prev),
        out_specs=[new(width), out_block, out_block],
        out_shape=[jax.ShapeDtypeStruct((1, ns * ts, width), BF16), out_cache, out_cache],
        input_output_aliases={n_in + i: 1 + i for i in range(len(prev))},
        compiler_params=_params(2), name="a_attn_sample")(
            q, k, v, knf, vnf, cache_k, cache_v, g, bias_s, *prev)
    ys = _out_proj(xs2, h, wo).reshape(xs.shape)
    return yp, ys, kf, vf, k_all, v_all


def _layer_b(xp, xs, ck, cv, norm_g, w_in, q_g, k_g, sinks, w_out):
    qw = B_HEADS * B_HEAD_DIM
    kw = B_KV_HEADS * B_HEAD_DIM
    w = w_in.astype(BF16)
    wo = w_out.astype(BF16)
    gn = norm_g.reshape(1, -1)
    qg = jnp.tile(q_g.reshape(1, -1), (1, 2))
    kg = jnp.tile(k_g.reshape(1, -1), (1, 2))
    sinks2 = sinks.astype(F32) * LOG2E

    def proj(x, tail):
        return _proj_ab(_proj_b_kernel, x, gn, w, qg, kg, q_width=qw, kv_width=4 * kw, g_width=qw,
                        scale=B_HEAD_DIM ** -0.5 * LOG2E, tail=tail, name="proj_b")

    slopes = 2.0 ** (-8.0 * jnp.arange(1, B_HEADS + 1, dtype=F32) / B_HEADS)

    def bias_base(dist):
        return -slopes[:, None] * jnp.abs(dist).astype(F32)[None, :]

    def expand_cache(c):
        n, rows, _ = c.shape
        c4 = c.reshape(n, rows, B_KV_HEADS, 1, B_HEAD_DIM)
        z = jnp.zeros_like(c4)
        lo = jnp.concatenate([c4, z], axis=-1)
        hi = jnp.concatenate([z, c4], axis=-1)
        return jnp.concatenate([lo, hi], axis=3).reshape(n, rows, 4 * kw).astype(BF16)

    heads = tuple(((t, 2 * (t // 2)), (t, 2 * (t // 2) + 1)) for t in range(qw // LANES))
    yp, kf, vf = _band_prompt(proj, xp, wo, reach=B_REACH, band_chunks=B_BAND_CHUNKS, heads=heads,
                              bias_base=bias_base, sinks=sinks2, name="b")
    n = xp.shape[0]
    head_shape = ck.shape[2:]

    ns, ts, d = xs.shape
    past = ck.shape[1]
    xs2 = xs.reshape(1, ns * ts, d)
    q, k, v, g, kf_s, vf_s = proj(xs2, None)
    ckf = ck.reshape(ns, past, -1)
    cvf = cv.reshape(ns, past, -1)
    cache_map = lambda b, i: (i, 0, 0)
    new_map = lambda b, i: (0, i, 0)
    bias_s = _band_bias(bias_base, ts, past + ts, past, None)
    h = _window_attn(q, [(expand_cache(ckf), past, cache_map), (k, ts, new_map)],
                     [(expand_cache(cvf), past, cache_map), (v, ts, new_map)], g, bias_s, sinks2,
                     grid=(1, ns), q_map=new_map, q_block=ts, heads=heads, need=(0, 0),
                     name="b_attn_sample")
    ys = _out_proj(xs2, h, wo).reshape(xs.shape)
    k_all = jnp.concatenate([ckf, kf_s.reshape(ns, ts, -1)], axis=1)[:, -B_REACH:]
    v_all = jnp.concatenate([cvf, vf_s.reshape(ns, ts, -1)], axis=1)[:, -B_REACH:]
    return (yp, ys, kf.reshape((n, B_REACH) + head_shape), vf.reshape((n, B_REACH) + head_shape),
            k_all.reshape((ns, B_REACH) + head_shape), v_all.reshape((ns, B_REACH) + head_shape))


def _rope_tables(pos):
    half = C_ROPE // 2
    inv = ROPE_THETA ** (-jnp.arange(half, dtype=F32) / half)
    ang = pos.astype(F32)[:, None] * inv[None, :]
    cos, sin = jnp.cos(ang), jnp.sin(ang)
    rows = pos.shape[0]
    ones = jnp.ones((rows, C_NOPE), F32)
    zeros = lambda width: jnp.zeros((rows, width), F32)
    pad = LANES - C_QK
    cos_t = jnp.concatenate([ones, cos, cos, zeros(pad)], axis=1)
    sin_t = jnp.concatenate([zeros(C_NOPE), -sin, sin, zeros(pad)], axis=1)
    return cos_t, sin_t


def _rope_padded(x):
    half = C_ROPE // 2
    zeros = jnp.zeros(x.shape[:-1] + (LANES - C_QK - half,), x.dtype)
    return jnp.concatenate([x, x[..., C_NOPE:C_NOPE + half], zeros], axis=-1)


def _proj_c(x, pos, gn, w, qag, wqb, kvag, qg, krg):
    n, s, d = x.shape
    tm = _row_tile(s)
    per_seq = pos.shape[0] // tm
    tables = _rope_tables(pos)
    table_spec = pl.BlockSpec((tm, LANES), lambda b, i: (i % per_seq, 0))
    consts = (gn, w, qag, wqb, kvag, qg, krg)
    return pl.pallas_call(
        functools.partial(_proj_c_kernel, scale=C_QK ** -0.5 * LOG2E),
        grid=(n, s // tm),
        in_specs=[_rows(tm, d)] + [_const_spec(c.shape) for c in consts] + [table_spec] * 2,
        out_specs=[_rows(tm, C_HEADS * LANES), _rows(tm, C_KV_LORA), _rows(tm, LANES),
                   _rows(tm, C_HEADS * C_V)],
        out_shape=[jax.ShapeDtypeStruct((n, s, C_HEADS * LANES), BF16),
                   jax.ShapeDtypeStruct((n, s, C_KV_LORA), F32),
                   jax.ShapeDtypeStruct((n, s, LANES), F32),
                   jax.ShapeDtypeStruct((n, s, C_HEADS * C_V), F32)],
        compiler_params=_params(2), name="proj_c")(x, *consts, *tables)


def _single(shape, index_map):
    return pl.BlockSpec(shape, index_map, pipeline_mode=pl.Buffered(1))


def _c_attn(q, ckv, kr, w, kg, g):
    n, s, _ = q.shape
    qb = min(ATTN_Q_BLOCK, s)
    kb = qb
    assert s % qb == 0 and qb % kb == 0 and kb % CHUNK == 0
    per_q = qb // kb
    row_chunk = jnp.arange(qb)[None, :, None] // CHUNK
    col_chunk = (jnp.arange(per_q)[:, None, None] * kb + jnp.arange(kb)[None, None, :]) // CHUNK
    mask = jnp.where(col_chunk <= row_chunk, 0.0, NEG_INF).astype(F32)
    width = C_HEADS * LANES
    whole = lambda w_: _single((1, s, w_), lambda b, i: (b, 0, 0))
    const = lambda a: _single(a.shape, lambda b, i: (0,) * a.ndim)
    return pl.pallas_call(
        functools.partial(_c_attn_kernel, kb=kb, expand_tile=qb), grid=(n, s // qb),
        in_specs=[_rows(qb, width), whole(C_KV_LORA), whole(LANES), const(w), const(kg),
                  _rows(qb, g.shape[-1]), const(mask)],
        out_specs=_rows(qb, g.shape[-1]),
        out_shape=jax.ShapeDtypeStruct(g.shape, BF16),
        scratch_shapes=[pltpu.VMEM((s, width), BF16), pltpu.VMEM((s, width), BF16),
                        pltpu.VMEM((C_HEADS, qb, LANES), F32), pltpu.VMEM((C_HEADS, qb, LANES), F32)],
        compiler_params=_params(2), name="c_attn_prompt")(q, ckv, kr, w, kg, g, mask)


def _c_attn_sample(q, cache_kv, cache_kr, first, new_kv, new_kr, w, kg, g, ns):
    past = cache_kv.shape[1]
    ts = q.shape[1] // ns
    width = C_HEADS * LANES
    tile = math.gcd(past, ATTN_Q_BLOCK)
    place = jnp.pad(jnp.eye(C_ROPE, dtype=BF16), ((0, 0), (C_NOPE, LANES - C_QK)))
    new = lambda w_: pl.BlockSpec((1, ts, w_), lambda b, i: (0, i, 0))
    cached = lambda w_: pl.BlockSpec((1, past, w_), lambda b, i: (first + i, 0, 0))
    return pl.pallas_call(
        functools.partial(_c_sample_kernel, expand_tile=tile), grid=(1, ns),
        in_specs=[new(width), cached(C_KV_LORA), cached(C_ROPE), new(C_KV_LORA), new(LANES),
                  _const_spec(w.shape), _const_spec(kg.shape), _const_spec(place.shape),
                  new(g.shape[-1])],
        out_specs=new(g.shape[-1]),
        out_shape=jax.ShapeDtypeStruct(g.shape, BF16),
        scratch_shapes=[pltpu.VMEM((past + ts, width), BF16), pltpu.VMEM((past + ts, width), BF16)],
        compiler_params=_params(2), name="c_attn_sample")(
            q, cache_kv, cache_kr, new_kv, new_kr, w, kg, place, g)


def _layer_c(xp, xs, cache_kv, cache_kr, layer, norm_g, w_in, qa_g, w_qb, kva_g, w_kvb, q_g, k_g,
             w_out):
    n, s, d = xp.shape
    o1 = C_Q_LORA
    o2 = o1 + C_KV_LORA
    o3 = o2 + C_ROPE
    kr_cols = _rope_padded(jnp.pad(w_in[:, o2:o3], ((0, 0), (C_NOPE, 0))))
    w = jnp.concatenate([w_in[:, :o2], w_in[:, o3:], kr_cols], axis=1).astype(BF16)
    wqb = _rope_padded(w_qb.reshape(C_Q_LORA, C_HEADS, C_QK))
    wqb = wqb.reshape(C_Q_LORA, C_HEADS * LANES).astype(BF16)
    kvb = w_kvb.reshape(C_KV_LORA, C_HEADS, C_NOPE + C_V)
    wk = jnp.pad(kvb[..., :C_NOPE], ((0, 0), (0, 0), (0, LANES - C_NOPE)))
    wv = kvb[..., C_NOPE:].reshape(C_KV_LORA, C_HEADS // 2, 2, C_V)
    zv = jnp.zeros_like(wv[:, :, 0])
    wv = jnp.stack([jnp.concatenate([wv[:, :, 0], zv], axis=-1),
                    jnp.concatenate([zv, wv[:, :, 1]], axis=-1)], axis=2)
    wkv = jnp.concatenate([wk.reshape(C_KV_LORA, -1), wv.reshape(C_KV_LORA, -1)], axis=1).astype(BF16)
    wo = w_out.astype(BF16)

    gn, qag, kvag = norm_g.reshape(1, -1), qa_g.reshape(1, -1), kva_g.reshape(1, -1)
    qg = _rope_padded(q_g.reshape(1, -1))
    krg = _rope_padded(jnp.pad(k_g[C_NOPE:].reshape(1, -1), ((0, 0), (C_NOPE, 0))))
    kg = jnp.pad(k_g[:C_NOPE].reshape(1, -1), ((0, 0), (0, LANES - C_NOPE)))

    q, ckv_p, kr_p, g = _proj_c(xp, jnp.arange(s), gn, w, qag, wqb, kvag, qg, krg)
    yp = _out_proj(xp, _c_attn(q, ckv_p, kr_p, wkv, kg, g), wo)

    n_layers, ns, past = cache_kv.shape[:3]
    ts = xs.shape[1]
    xs2 = xs.reshape(1, ns * ts, d)
    pos = past + jnp.arange(ts)
    q, ckv_s, kr_s, g = _proj_c(xs2, jnp.tile(pos, ns), gn, w, qag, wqb, kvag, qg, krg)
    h = _c_attn_sample(q, cache_kv.reshape(n_layers * ns, past, C_KV_LORA),
                       cache_kr.reshape(n_layers * ns, past, C_ROPE), layer * ns,
                       ckv_s, kr_s, wkv, kg, g, ns)
    ys = _out_proj(xs2, h, wo).reshape(xs.shape)
    rope = slice(C_NOPE, C_QK)
    return (yp, ys, ckv_p, kr_p[..., rope], ckv_s.reshape(ns, ts, C_KV_LORA),
            kr_s.reshape(ns, ts, LANES)[..., rope])


def kernel(x_prompt, x_sample, cache_a_k, cache_a_v, cache_b_k, cache_b_v, cache_c_kv, cache_c_kr,
           a_norm, a_w_in, a_q_norm, a_k_norm, a_rel_bias, a_w_out,
           b_norm, b_w_in, b_q_norm, b_k_norm, b_sinks, b_w_out,
           c_norm, c_w_in, c_q_a_norm, c_w_qb, c_kv_a_norm, c_w_kvb, c_q_norm, c_k_norm, c_w_out):
    depth = a_norm.shape[0] + b_norm.shape[0] + c_norm.shape[0]
    xp, xs = x_prompt, x_sample
    outs = [[] for _ in range(12)]
    a_k_all = a_v_all = None
    for layer in range(depth):
        j, kind = divmod(layer, N_MIXERS)
        if kind == 0:
            res = _layer_a(xp, xs, cache_a_k, cache_a_v, j, a_k_all, a_v_all, a_norm[j], a_w_in[j],
                           a_q_norm[j], a_k_norm[j], a_rel_bias[j], a_w_out[j])
            a_k_all, a_v_all = res[4], res[5]
        elif kind == 1:
            res = _layer_b(xp, xs, cache_b_k[j], cache_b_v[j], b_norm[j], b_w_in[j], b_q_norm[j],
                           b_k_norm[j], b_sinks[j], b_w_out[j])
        else:
            res = _layer_c(xp, xs, cache_c_kv, cache_c_kr, j, c_norm[j], c_w_in[j], c_q_a_norm[j],
                           c_w_qb[j], c_kv_a_norm[j], c_w_kvb[j], c_q_norm[j], c_k_norm[j],
                           c_w_out[j])
        xp, xs = res[0], res[1]
        for slot in range(4):
            outs[4 * kind + slot].append(res[2 + slot])
    outs[2], outs[3] = None, None
    stacked = [None if o is None else jnp.stack(o) for o in outs]
    stacked[2], stacked[3] = a_k_all, a_v_all
    order = [0, 1, 4, 5, 8, 9, 2, 3, 6, 7, 10, 11]
    return (xp, xs) + tuple(stacked[i] for i in order)
```

```python
import functools
import math

import jax
import jax.numpy as jnp
from jax import lax
from jax.experimental import pallas as pl
from jax.experimental.pallas import tpu as pltpu

F32 = jnp.float32
BF16 = jnp.bfloat16

LANES = 128
CHUNK = 64
NORM_EPS = 1e-6
NEG_INF = -1e30
LOG2E = math.log2(math.e)
N_MIXERS = 3

A_HEADS = 8
A_HEAD_DIM = 128
A_BAND_CHUNKS = 9
A_REL_CLIP = 128
A_REACH = (A_BAND_CHUNKS - 1) * CHUNK

B_HEADS = 16
B_KV_HEADS = 4
B_HEAD_DIM = 64
B_BAND_CHUNKS = 3
B_REACH = (B_BAND_CHUNKS - 1) * CHUNK

C_HEADS = 16
C_NOPE = 64
C_ROPE = 32
C_QK = C_NOPE + C_ROPE
C_V = 64
C_Q_LORA = 512
C_KV_LORA = 256
ROPE_THETA = 10000.0

ROW_TILE = 512
OUT_ROW_TILE = 1024
ATTN_Q_BLOCK = 256
VMEM_LIMIT = 56 * 1024 * 1024


def _params(n_axes):
    return pltpu.CompilerParams(dimension_semantics=("arbitrary",) * n_axes,
                                vmem_limit_bytes=VMEM_LIMIT)


def _const_spec(shape):
    zeros = (0,) * len(shape)
    return pl.BlockSpec(shape, lambda *_: zeros)


def _rms(x, width):
    return x * lax.rsqrt(jnp.sum(x * x, axis=-1, keepdims=True) * (1.0 / width) + NORM_EPS)


def _normed_input(x_ref, gn_ref):
    x = x_ref[0]
    return (_rms(x, x.shape[-1]) * gn_ref[...]).astype(BF16)


def _tile(val, t):
    return val[:, t * LANES:(t + 1) * LANES]


def _silu(g):
    return g / (1.0 + jnp.exp(-g))


def _qk(q, k):
    return lax.dot_general(q, k, (((1,), (1,)), ((), ())), preferred_element_type=F32)


def _proj_a_kernel(x_ref, gn_ref, w_ref, qg_ref, kg_ref,
                   q_ref, k_ref, v_ref, g_ref, kf_ref, vf_ref, *, scale, tail):
    xn = _normed_input(x_ref, gn_ref)
    width = A_HEADS * A_HEAD_DIM
    rows = xn.shape[0]
    zq = jnp.dot(xn, w_ref[:, 0:width], preferred_element_type=F32)
    for h in range(A_HEADS):
        qn = _rms(_tile(zq, h), A_HEAD_DIM) * qg_ref[...]
        q_ref[0, :, h * LANES:(h + 1) * LANES] = (qn * scale).astype(BF16)
    zk = jnp.dot(xn, w_ref[:, width:2 * width], preferred_element_type=F32)
    for h in range(A_HEADS):
        kn = _rms(_tile(zk, h), A_HEAD_DIM) * kg_ref[...]
        k_ref[0, :, h * LANES:(h + 1) * LANES] = kn.astype(BF16)
        kf_ref[0, :, h * LANES:(h + 1) * LANES] = kn[rows - tail:, :]
    zv = jnp.dot(xn, w_ref[:, 2 * width:3 * width], preferred_element_type=F32)
    v_ref[0] = zv.astype(BF16)
    vf_ref[0] = zv[rows - tail:, :]
    g_ref[0] = jnp.dot(xn, w_ref[:, 3 * width:4 * width], preferred_element_type=F32)


def _half_rms(z, lo, width):
    sq = z * z
    s_lo = jnp.sum(jnp.where(lo, sq, 0.0), axis=-1, keepdims=True)
    s_hi = jnp.sum(jnp.where(lo, 0.0, sq), axis=-1, keepdims=True)
    r = jnp.where(lo, lax.rsqrt(s_lo * (1.0 / width) + NORM_EPS),
                  lax.rsqrt(s_hi * (1.0 / width) + NORM_EPS))
    return z * r


def _expand_halves(t, lo, out_ref, u):
    rolled = pltpu.roll(t, B_HEAD_DIM, 1)
    zero = jnp.zeros_like(t)
    tiles = (jnp.where(lo, t, zero), jnp.where(lo, zero, rolled),
             jnp.where(lo, rolled, zero), jnp.where(lo, zero, t))
    for i, val in enumerate(tiles):
        c = 4 * u + i
        out_ref[0, :, c * LANES:(c + 1) * LANES] = val.astype(BF16)


def _proj_b_kernel(x_ref, gn_ref, w_ref, qg_ref, kg_ref,
                   q_ref, k_ref, v_ref, g_ref, kf_ref, vf_ref, *, scale, tail):
    xn = _normed_input(x_ref, gn_ref)
    rows = xn.shape[0]
    qw = B_HEADS * B_HEAD_DIM
    kw = B_KV_HEADS * B_HEAD_DIM
    lo = lax.broadcasted_iota(jnp.int32, (rows, LANES), 1) < B_HEAD_DIM
    zq = jnp.dot(xn, w_ref[:, 0:qw], preferred_element_type=F32)
    for t in range(qw // LANES):
        qn = _half_rms(_tile(zq, t), lo, B_HEAD_DIM) * qg_ref[...]
        q_ref[0, :, t * LANES:(t + 1) * LANES] = (qn * scale).astype(BF16)
    zk = jnp.dot(xn, w_ref[:, qw:qw + kw], preferred_element_type=F32)
    for u in range(kw // LANES):
        kn = _half_rms(_tile(zk, u), lo, B_HEAD_DIM) * kg_ref[...]
        kf_ref[0, :, u * LANES:(u + 1) * LANES] = kn[rows - tail:, :]
        _expand_halves(kn, lo, k_ref, u)
    zv = jnp.dot(xn, w_ref[:, qw + kw:qw + 2 * kw], preferred_element_type=F32)
    vf_ref[0] = zv[rows - tail:, :]
    for u in range(kw // LANES):
        _expand_halves(_tile(zv, u), lo, v_ref, u)
    g_ref[0] = jnp.dot(xn, w_ref[:, qw + 2 * kw:2 * qw + 2 * kw], preferred_element_type=F32)


def _rotate(x, cos_ref, sin_ref):
    return x * cos_ref[...] + pltpu.roll(x, LANES - C_ROPE // 2, 1) * sin_ref[...]


def _proj_c_kernel(x_ref, gn_ref, w_ref, qag_ref, wqb_ref, kvag_ref, qg_ref, krg_ref,
                   cos_ref, sin_ref, q_ref, ckv_ref, kr_ref, g_ref, *, scale):
    xn = _normed_input(x_ref, gn_ref)
    rows = xn.shape[0]
    o1 = C_Q_LORA
    o2 = o1 + C_KV_LORA
    o3 = o2 + C_HEADS * C_V
    lane = lax.broadcasted_iota(jnp.int32, (rows, LANES), 1)
    nope = lane < C_NOPE
    rope = (lane >= C_NOPE) & (lane < C_QK)
    za = jnp.dot(xn, w_ref[:, 0:o1], preferred_element_type=F32)
    qa = (_rms(za, C_Q_LORA) * qag_ref[...]).astype(BF16)
    zq = jnp.dot(qa, wqb_ref[...], preferred_element_type=F32)
    for h in range(C_HEADS):
        z = _tile(zq, h)
        sq = z * z
        s_n = jnp.sum(jnp.where(nope, sq, 0.0), axis=-1, keepdims=True)
        s_r = jnp.sum(jnp.where(rope, sq, 0.0), axis=-1, keepdims=True)
        r = jnp.where(nope, lax.rsqrt(s_n * (1.0 / C_NOPE) + NORM_EPS),
                      lax.rsqrt(s_r * (1.0 / C_ROPE) + NORM_EPS))
        qn = _rotate(z * r * qg_ref[...], cos_ref, sin_ref)
        q_ref[0, :, h * LANES:(h + 1) * LANES] = (qn * scale).astype(BF16)
    zc = jnp.dot(xn, w_ref[:, o1:o2], preferred_element_type=F32)
    ckv_ref[0] = _rms(zc, C_KV_LORA) * kvag_ref[...]
    g_ref[0] = jnp.dot(xn, w_ref[:, o2:o3], preferred_element_type=F32)
    zr = jnp.dot(xn, w_ref[:, o3:o3 + LANES], preferred_element_type=F32)
    s_k = jnp.sum(jnp.where(rope, zr * zr, 0.0), axis=-1, keepdims=True)
    krn = zr * lax.rsqrt(s_k * (1.0 / C_ROPE) + NORM_EPS) * krg_ref[...]
    kr_ref[0] = _rotate(krn, cos_ref, sin_ref)


def _expand_rows(ckv, kr, w_ref, kg_ref, k_scr, v_scr, rows_at):
    ckv = ckv.astype(BF16)
    width = C_HEADS * LANES
    lane = lax.broadcasted_iota(jnp.int32, (ckv.shape[0], LANES), 1)
    zk = jnp.dot(ckv, w_ref[:, 0:width], preferred_element_type=F32)
    for h in range(C_HEADS):
        kn = _rms(_tile(zk, h), C_NOPE) * kg_ref[...] + kr
        k_scr[rows_at, h * LANES:(h + 1) * LANES] = kn.astype(BF16)
    zv = jnp.dot(ckv, w_ref[:, width:2 * width], preferred_element_type=F32)
    for h in range(C_HEADS):
        ones_lane = C_V if h % 2 == 0 else 0
        v_scr[rows_at, h * LANES:(h + 1) * LANES] = jnp.where(lane == ones_lane, 1.0,
                                                              _tile(zv, h)).astype(BF16)


def _out_proj_kernel(x_ref, h_ref, w_ref, y_ref):
    y_ref[0] = x_ref[0] + jnp.dot(h_ref[0], w_ref[...], preferred_element_type=F32)


def _row_reduce(pieces, combine, reduce):
    tiles, narrow = [], []
    for piece in pieces:
        width = piece.shape[1]
        if width % LANES == 0:
            tiles += [piece[:, i * LANES:(i + 1) * LANES] for i in range(width // LANES)]
        else:
            narrow.append(piece)
    folded = ([functools.reduce(combine, tiles)] if tiles else []) + narrow
    return functools.reduce(combine, [reduce(x, axis=-1, keepdims=True) for x in folded])


def _softmax_parts(scores, sink):
    m = _row_reduce(scores, jnp.maximum, jnp.max)
    if sink is not None:
        m = jnp.maximum(m, sink)
    probs = [jnp.exp2(s - m) for s in scores]
    denom = _row_reduce(probs, jnp.add, jnp.sum)
    if sink is not None:
        denom = denom + jnp.exp2(sink - m)
    return probs, denom


def _window_attn_kernel(*refs, n_pieces, heads, need, has_bias, has_sinks):
    q_ref = refs[0]
    k_refs = refs[1:1 + n_pieces]
    v_refs = refs[1 + n_pieces:1 + 2 * n_pieces]
    pos = 1 + 2 * n_pieces
    g_ref = refs[pos]
    pos += 1
    bias_ref = None
    if has_bias:
        bias_ref = refs[pos]
        pos += 1
    sink_ref = None
    if has_sinks:
        sink_ref = refs[pos]
        pos += 1
    h_ref = refs[pos]

    step = pl.program_id(1)
    rows = q_ref.shape[1]
    lo = lax.broadcasted_iota(jnp.int32, (rows, LANES), 1) < LANES // 2
    pen = [jnp.where(step >= nd, 0.0, NEG_INF).astype(F32) if nd > 0 else None for nd in need]
    for t in range(len(heads)):
        acc = None
        for sub, (q_t, kv_t) in enumerate(heads[t]):
            head = t * len(heads[t]) + sub
            qt = q_ref[0, :, q_t * LANES:(q_t + 1) * LANES]
            scores = []
            off = 0
            for j in range(n_pieces):
                kj = k_refs[j][0, :, kv_t * LANES:(kv_t + 1) * LANES]
                s = _qk(qt, kj)
                if has_bias:
                    s = s + bias_ref[head, :, off:off + kj.shape[0]]
                if pen[j] is not None:
                    s = s + pen[j]
                scores.append(s)
                off += kj.shape[0]
            probs, denom = _softmax_parts(scores, sink_ref[head] if has_sinks else None)
            o = None
            for j in range(n_pieces):
                vj = v_refs[j][0, :, kv_t * LANES:(kv_t + 1) * LANES]
                pv = jnp.dot(probs[j].astype(BF16), vj, preferred_element_type=F32)
                o = pv if o is None else o + pv
            o = o / denom
            acc = o if acc is None else jnp.where(lo, acc, o)
        g = g_ref[0, :, t * LANES:(t + 1) * LANES]
        h_ref[0, :, t * LANES:(t + 1) * LANES] = (acc * _silu(g)).astype(BF16)


def _a_sample_kernel(q_ref, kn_ref, vn_ref, knf_ref, vnf_ref, ck_ref, cv_ref, g_ref, bias_ref,
                     *rest, slot):
    h_ref, ok_ref, ov_ref = rest[-3:]
    past = ck_ref.shape[2]
    new = q_ref.shape[1]
    reach = ok_ref.shape[2]
    drop = past + new - reach
    for other in range(ok_ref.shape[0]):
        if other != slot:
            ok_ref[other] = jnp.zeros(ok_ref.shape[1:], F32)
            ov_ref[other] = jnp.zeros(ov_ref.shape[1:], F32)
    k_heads = jnp.swapaxes(ck_ref[0, 0], 0, 1)
    v_heads = jnp.swapaxes(cv_ref[0, 0], 0, 1)
    k_out, v_out = [], []
    for h in range(A_HEADS):
        cols = slice(h * LANES, (h + 1) * LANES)
        kc = k_heads[h]
        vc = v_heads[h]
        qh = q_ref[0, :, cols]
        s_c = _qk(qh, kc.astype(BF16)) + bias_ref[h, :, 0:past]
        s_n = _qk(qh, kn_ref[0, :, cols]) + bias_ref[h, :, past:past + new]
        (p_c, p_n), denom = _softmax_parts([s_c, s_n], None)
        o = (jnp.dot(p_c.astype(BF16), vc.astype(BF16), preferred_element_type=F32)
             + jnp.dot(p_n.astype(BF16), vn_ref[0, :, cols], preferred_element_type=F32)) / denom
        h_ref[0, :, cols] = (o * _silu(g_ref[0, :, cols])).astype(BF16)
        k_out.append(jnp.concatenate([kc[drop:, :], knf_ref[0, :, cols]], axis=0))
        v_out.append(jnp.concatenate([vc[drop:, :], vnf_ref[0, :, cols]], axis=0))
    ok_ref[slot, 0] = jnp.swapaxes(jnp.stack(k_out), 0, 1)
    ov_ref[slot, 0] = jnp.swapaxes(jnp.stack(v_out), 0, 1)


def _online_update(c, q, kj, vj, bias, m_scr, acc_scr):
    s = _qk(q, kj)
    if bias is not None:
        s = s + bias
    parts = [s[:, i * LANES:(i + 1) * LANES] for i in range(s.shape[1] // LANES)]
    m_old = m_scr[c]
    m_new = jnp.maximum(m_old, jnp.max(functools.reduce(jnp.maximum, parts), axis=-1,
                                       keepdims=True))
    m_scr[c] = m_new
    p = jnp.concatenate([jnp.exp2(x - m_new).astype(BF16) for x in parts], axis=1)
    acc_scr[c] = (acc_scr[c] * jnp.exp2(m_old - m_new)
                  + jnp.dot(p, vj, preferred_element_type=F32))


def _band_attn_kernel(*refs, n_pieces, heads, need, has_sinks):
    q_ref = refs[0]
    k_refs = refs[1:1 + n_pieces]
    v_refs = refs[1 + n_pieces:1 + 2 * n_pieces]
    g_ref, bias_ref = refs[1 + 2 * n_pieces:3 + 2 * n_pieces]
    pos = 3 + 2 * n_pieces
    sink_ref = None
    if has_sinks:
        sink_ref = refs[pos]
        pos += 1
    x_ref, wo_ref, y_ref, s_scr, m_scr, h_scr = refs[pos:pos + 6]

    step = pl.program_id(1)
    rows = q_ref.shape[1]
    lo = lax.broadcasted_iota(jnp.int32, (rows, LANES), 1) < LANES // 2
    flat = [(t, sub, q_t, kv_t) for t in range(len(heads)) for sub, (q_t, kv_t) in enumerate(heads[t])]
    per_tile = len(heads[0])

    offsets = [sum(k_refs[i].shape[1] for i in range(j)) for j in range(n_pieces)]

    for j in range(n_pieces):
        width = k_refs[j].shape[1]
        pen = jnp.where(step >= need[j], 0.0, NEG_INF).astype(F32) if need[j] > 0 else None
        for t, sub, q_t, kv_t in flat:
            c = t * per_tile + sub
            s = _qk(q_ref[0, :, q_t * LANES:(q_t + 1) * LANES],
                    k_refs[j][0, :, kv_t * LANES:(kv_t + 1) * LANES])
            s = s + bias_ref[c, :, offsets[j]:offsets[j] + width]
            if pen is not None:
                s = s + pen
            s_scr[c, :, offsets[j]:offsets[j] + width] = s
            tiles = [s[:, i * LANES:(i + 1) * LANES] for i in range(width // LANES)]
            if j > 0:
                tiles.append(m_scr[c])
            m_scr[c] = functools.reduce(jnp.maximum, tiles)

    slot0 = jnp.minimum(step, 0)
    for t in range(len(heads)):
        out = None
        for sub, (_, kv_t) in enumerate(heads[t]):
            c = t * per_tile + sub
            m = jnp.max(m_scr[c + slot0], axis=-1, keepdims=True)
            if has_sinks:
                m = jnp.maximum(m, sink_ref[c])
            lanes_sum = None
            o = None
            for j in range(n_pieces):
                width = k_refs[j].shape[1]
                p = jnp.exp2(s_scr[c + slot0, :, offsets[j]:offsets[j] + width] - m)
                for i in range(width // LANES):
                    tile = p[:, i * LANES:(i + 1) * LANES]
                    lanes_sum = tile if lanes_sum is None else lanes_sum + tile
                pv = jnp.dot(p.astype(BF16), v_refs[j][0, :, kv_t * LANES:(kv_t + 1) * LANES],
                             preferred_element_type=F32)
                o = pv if o is None else o + pv
            denom = jnp.sum(lanes_sum, axis=-1, keepdims=True)
            if has_sinks:
                denom = denom + jnp.exp2(sink_ref[c] - m)
            o = o / denom
            out = o if out is None else jnp.where(lo, out, o)
        g = g_ref[0, :, t * LANES:(t + 1) * LANES]
        h_scr[:, t * LANES:(t + 1) * LANES] = (out * _silu(g)).astype(BF16)

    y_ref[0] = x_ref[0] + jnp.dot(h_scr[...], wo_ref[...], preferred_element_type=F32)


def _c_attn_kernel(q_ref, ckv_ref, kr_ref, w_ref, kg_ref, g_ref, mask_ref, h_ref,
                   k_scr, v_scr, m_scr, acc_scr, *, kb, expand_tile):
    step = pl.program_id(1)
    rows = q_ref.shape[1]
    lane = lax.broadcasted_iota(jnp.int32, (rows, LANES), 1)
    lo = lane < C_V
    per_q = rows // kb

    @pl.when(step == 0)
    def _expand_sequence():
        def expand(r, carry):
            at = pl.ds(pl.multiple_of(r * expand_tile, expand_tile), expand_tile)
            _expand_rows(ckv_ref[0, at, :], kr_ref[0, at, :], w_ref, kg_ref, k_scr, v_scr, at)
            return carry
        lax.fori_loop(0, ckv_ref.shape[1] // expand_tile, expand, 0)

    m_scr[...] = jnp.full(m_scr.shape, NEG_INF, F32)
    acc_scr[...] = jnp.zeros(acc_scr.shape, F32)

    def block(j, masked):
        start = pl.multiple_of(j * kb, kb)
        for c in range(C_HEADS):
            cols = slice(c * LANES, (c + 1) * LANES)
            _online_update(c, q_ref[0, :, cols], k_scr[pl.ds(start, kb), cols],
                           v_scr[pl.ds(start, kb), cols],
                           None if masked is None else mask_ref[masked], m_scr, acc_scr)

    def body(j, carry):
        block(j, None)
        return carry

    lax.fori_loop(0, step * per_q, body, 0)
    for d in range(per_q):
        block(step * per_q + d, d)

    for t in range(C_HEADS // 2):
        a_e = acc_scr[2 * t]
        a_o = acc_scr[2 * t + 1]
        l_e = jnp.sum(jnp.where(lane == C_V, a_e, 0.0), axis=-1, keepdims=True)
        l_o = jnp.sum(jnp.where(lane == 0, a_o, 0.0), axis=-1, keepdims=True)
        g = g_ref[0, :, t * LANES:(t + 1) * LANES]
        o = jnp.where(lo, a_e / l_e, a_o / l_o)
        h_ref[0, :, t * LANES:(t + 1) * LANES] = (o * _silu(g)).astype(BF16)


def _c_sample_kernel(q_ref, ckv_ref, ckr_ref, nkv_ref, nkr_ref, w_ref, kg_ref, place_ref, g_ref,
                     h_ref, k_scr, v_scr, *, expand_tile):
    past = ckv_ref.shape[1]
    new = q_ref.shape[1]
    lo = lax.broadcasted_iota(jnp.int32, (new, LANES), 1) < C_V

    def expand(r, carry):
        at = pl.ds(pl.multiple_of(r * expand_tile, expand_tile), expand_tile)
        kr = jnp.dot(ckr_ref[0, at, :].astype(BF16), place_ref[...], preferred_element_type=F32)
        _expand_rows(ckv_ref[0, at, :], kr, w_ref, kg_ref, k_scr, v_scr, at)
        return carry
    lax.fori_loop(0, past // expand_tile, expand, 0)
    _expand_rows(nkv_ref[0], nkr_ref[0], w_ref, kg_ref, k_scr, v_scr, pl.ds(past, new))

    for t in range(C_HEADS // 2):
        acc = None
        for c in (2 * t, 2 * t + 1):
            cols = slice(c * LANES, (c + 1) * LANES)
            q = q_ref[0, :, cols]
            s_c = _qk(q, k_scr[0:past, cols])
            s_n = _qk(q, k_scr[past:past + new, cols])
            (p_c, p_n), denom = _softmax_parts([s_c, s_n], None)
            o = (jnp.dot(p_c.astype(BF16), v_scr[0:past, cols], preferred_element_type=F32)
                 + jnp.dot(p_n.astype(BF16), v_scr[past:past + new, cols],
                           preferred_element_type=F32)) / denom
            acc = o if acc is None else jnp.where(lo, acc, o)
        g = g_ref[0, :, t * LANES:(t + 1) * LANES]
        h_ref[0, :, t * LANES:(t + 1) * LANES] = (acc * _silu(g)).astype(BF16)


def _rows(tm, width):
    return pl.BlockSpec((1, tm, width), lambda b, i: (b, i, 0))


def _cache_out(n, s, tm, width, tail):
    if tail is None:
        return jax.ShapeDtypeStruct((n, s, width), F32), _rows(tm, width), tm
    assert tail <= tm and s % tm == 0
    spec = pl.BlockSpec((1, tail, width), lambda b, i: (b, 0, 0))
    return jax.ShapeDtypeStruct((n, tail, width), F32), spec, tail


def _row_tile(s):
    tm = min(ROW_TILE, s)
    assert s % tm == 0
    return tm


def _proj_ab(kernel, x, gn, w, qg, kg, *, q_width, kv_width, g_width, scale, tail, name):
    n, s, d = x.shape
    tm = _row_tile(s)
    cache_width = (w.shape[1] - q_width - g_width) // 2
    cache_shape, cache_spec, tail_rows = _cache_out(n, s, tm, cache_width, tail)
    return pl.pallas_call(
        functools.partial(kernel, scale=scale, tail=tail_rows),
        grid=(n, s // tm),
        in_specs=[_rows(tm, d), _const_spec(gn.shape), _const_spec(w.shape),
                  _const_spec(qg.shape), _const_spec(kg.shape)],
        out_specs=[_rows(tm, q_width), _rows(tm, kv_width), _rows(tm, kv_width), _rows(tm, g_width),
                   cache_spec, cache_spec],
        out_shape=[jax.ShapeDtypeStruct((n, s, q_width), BF16),
                   jax.ShapeDtypeStruct((n, s, kv_width), BF16),
                   jax.ShapeDtypeStruct((n, s, kv_width), BF16),
                   jax.ShapeDtypeStruct((n, s, g_width), F32), cache_shape, cache_shape],
        compiler_params=_params(2), name=name)(x, gn, w, qg, kg)


def _out_proj(x, h, w):
    n, s, d = x.shape
    tm = math.gcd(s, OUT_ROW_TILE)
    return pl.pallas_call(
        _out_proj_kernel, grid=(n, s // tm),
        in_specs=[_rows(tm, d), _rows(tm, h.shape[-1]), _const_spec(w.shape)],
        out_specs=_rows(tm, d), out_shape=jax.ShapeDtypeStruct((n, s, d), F32),
        compiler_params=_params(2), name="out_proj")(x, h, w)


def _window_attn(q, k_pieces, v_pieces, g, bias, sinks, *, grid, q_map, q_block, heads, need, name):
    width = len(heads) * LANES
    in_specs = [pl.BlockSpec((1, q_block, q.shape[-1]), q_map)]
    args = [q]
    for arr, rows, imap in list(k_pieces) + list(v_pieces):
        in_specs.append(pl.BlockSpec((1, rows, arr.shape[-1]), imap))
        args.append(arr)
    in_specs.append(pl.BlockSpec((1, q_block, width), q_map))
    args.append(g)
    if bias is not None:
        in_specs.append(_const_spec(bias.shape))
        args.append(bias)
    if sinks is not None:
        in_specs.append(pl.BlockSpec(memory_space=pltpu.SMEM))
        args.append(sinks)
    kernel = functools.partial(_window_attn_kernel, n_pieces=len(k_pieces), heads=heads, need=need,
                               has_bias=bias is not None, has_sinks=sinks is not None)
    return pl.pallas_call(
        kernel, grid=grid, in_specs=in_specs,
        out_specs=pl.BlockSpec((1, q_block, width), q_map),
        out_shape=jax.ShapeDtypeStruct(g.shape[:2] + (width,), BF16),
        compiler_params=_params(2), name=name)(*args)


def _band_bias(base, q_rows, k_rows, q_off, band_chunks):
    m = q_rows + k_rows - 1
    vec = base(q_off + (q_rows - 1) - jnp.arange(m)) * LOG2E
    flat = jnp.tile(vec, (1, q_rows))[:, q_rows - 1:q_rows - 1 + q_rows * (m - 1)]
    bias = flat.reshape(-1, q_rows, m - 1)[:, :, :k_rows]
    if band_chunks is None:
        return bias
    i = jnp.arange(q_rows)[:, None]
    j = jnp.arange(k_rows)[None, :]
    lag = (q_off + i) // CHUNK - j // CHUNK
    return jnp.where((lag >= 0) & (lag < band_chunks), bias, NEG_INF)


def _band_prompt_pieces(arr, q_block, reach):
    pieces, need = [], []
    if reach % q_block == 0:
        for r in range(reach // q_block, 0, -1):
            pieces.append((arr, q_block, lambda b, i, r=r: (b, jnp.maximum(i - r, 0), 0)))
            need.append(r)
    else:
        assert q_block % reach == 0
        ratio = q_block // reach
        pieces.append((arr, reach, lambda b, i: (b, jnp.maximum(i * ratio - 1, 0), 0)))
        need.append(1)
    pieces.append((arr, q_block, lambda b, i: (b, i, 0)))
    need.append(0)
    return pieces, tuple(need)


def _band_prompt(proj, xp, wo, *, reach, band_chunks, heads, bias_base, sinks, name):
    n, s, d = xp.shape
    qb = min(ATTN_Q_BLOCK, s)
    assert s % qb == 0 and s >= reach
    q, k, v, g, kf, vf = proj(xp, reach)
    kp, need = _band_prompt_pieces(k, qb, reach)
    vp, _ = _band_prompt_pieces(v, qb, reach)
    bias = _band_bias(bias_base, qb, reach + qb, reach, band_chunks)
    width = len(heads) * LANES
    n_slots = sum(len(hs) for hs in heads)
    pieces = kp + vp
    args = [q] + [arr for arr, _, _ in pieces] + [g, bias]
    in_specs = ([_rows(qb, q.shape[-1])]
                + [pl.BlockSpec((1, rows, arr.shape[-1]), imap) for arr, rows, imap in pieces]
                + [_rows(qb, width), _single(bias.shape, lambda b, i: (0, 0, 0))])
    if sinks is not None:
        in_specs.append(pl.BlockSpec(memory_space=pltpu.SMEM))
        args.append(sinks)
    in_specs += [_rows(qb, d), _single(wo.shape, lambda b, i: (0, 0))]
    args += [xp, wo]
    scratch = [pltpu.VMEM((n_slots, qb, reach + qb), F32), pltpu.VMEM((n_slots, qb, LANES), F32),
               pltpu.VMEM((qb, width), BF16)]
    yp = pl.pallas_call(
        functools.partial(_band_attn_kernel, n_pieces=len(kp), heads=heads, need=need,
                          has_sinks=sinks is not None),
        grid=(n, s // qb), in_specs=in_specs, out_specs=_rows(qb, d),
        out_shape=jax.ShapeDtypeStruct((n, s, d), F32),
        scratch_shapes=scratch,
        compiler_params=_params(2), name=name + "_attn_prompt")(*args)
    return yp, kf, vf


def _layer_a(xp, xs, cache_k, cache_v, layer, prev_k, prev_v, norm_g, w_in, q_g, k_g, rel, w_out):
    width = A_HEADS * A_HEAD_DIM
    w = w_in.astype(BF16)
    wo = w_out.astype(BF16)
    gn, qg, kg = norm_g.reshape(1, -1), q_g.reshape(1, -1), k_g.reshape(1, -1)

    def proj(x, tail):
        return _proj_ab(_proj_a_kernel, x, gn, w, qg, kg, q_width=width, kv_width=width,
                        g_width=width, scale=A_HEAD_DIM ** -0.5 * LOG2E, tail=tail, name="proj_a")

    def bias_base(dist):
        idx = jnp.clip(dist, -A_REL_CLIP, A_REL_CLIP) + A_REL_CLIP
        return rel.astype(F32)[:, idx]

    heads = tuple(((t, t),) for t in range(A_HEADS))
    yp, kf, vf = _band_prompt(proj, xp, wo, reach=A_REACH, band_chunks=A_BAND_CHUNKS, heads=heads,
                              bias_base=bias_base, sinks=None, name="a")
    n = xp.shape[0]
    head_shape = (A_HEADS, A_HEAD_DIM)
    kf = kf.reshape((n, A_REACH) + head_shape)
    vf = vf.reshape((n, A_REACH) + head_shape)

    n_layers, ns, past = cache_k.shape[:3]
    ts, d = xs.shape[1:]
    xs2 = xs.reshape(1, ns * ts, d)
    q, k, v, g, knf, vnf = proj(xs2, None)
    bias_s = _band_bias(bias_base, ts, past + ts, past, None)
    new = lambda w_: pl.BlockSpec((1, ts, w_), lambda b, i: (0, i, 0))
    cache_block = lambda rows: pl.BlockSpec((1, 1, rows) + head_shape,
                                            lambda b, i: (layer, i, 0, 0, 0))
    out_cache = jax.ShapeDtypeStruct((n_layers, ns, A_REACH) + head_shape, F32)
    prev = [] if prev_k is None else [prev_k, prev_v]
    if prev:
        out_block, slot = cache_block(A_REACH), 0
    else:
        out_block = pl.BlockSpec((n_layers, 1, A_REACH) + head_shape, lambda b, i: (0, i, 0, 0, 0))
        slot = layer
    n_in = 9
    h, k_all, v_all = pl.pallas_call(
        functools.partial(_a_sample_kernel, slot=slot), grid=(1, ns),
        in_specs=[new(width)] * 5 + [cache_block(past)] * 2 + [new(width), _const_spec(bias_s.shape)]
        + [pl.BlockSpec(memory_space=pl.ANY)] * len(prev),
        out_specs=[new(width), out_block, out_block],
        out_shape=[jax.ShapeDtypeStruct((1, ns * ts, width), BF16), out_cache, out_cache],
        input_output_aliases={n_in + i: 1 + i for i in range(len(prev))},
        compiler_params=_params(2), name="a_attn_sample")(
            q, k, v, knf, vnf, cache_k, cache_v, g, bias_s, *prev)
    ys = _out_proj(xs2, h, wo).reshape(xs.shape)
    return yp, ys, kf, vf, k_all, v_all


def _layer_b(xp, xs, ck, cv, norm_g, w_in, q_g, k_g, sinks, w_out):
    qw = B_HEADS * B_HEAD_DIM
    kw = B_KV_HEADS * B_HEAD_DIM
    w = w_in.astype(BF16)
    wo = w_out.astype(BF16)
    gn = norm_g.reshape(1, -1)
    qg = jnp.tile(q_g.reshape(1, -1), (1, 2))
    kg = jnp.tile(k_g.reshape(1, -1), (1, 2))
    sinks2 = sinks.astype(F32) * LOG2E

    def proj(x, tail):
        return _proj_ab(_proj_b_kernel, x, gn, w, qg, kg, q_width=qw, kv_width=4 * kw, g_width=qw,
                        scale=B_HEAD_DIM ** -0.5 * LOG2E, tail=tail, name="proj_b")

    slopes = 2.0 ** (-8.0 * jnp.arange(1, B_HEADS + 1, dtype=F32) / B_HEADS)

    def bias_base(dist):
        return -slopes[:, None] * jnp.abs(dist).astype(F32)[None, :]

    def expand_cache(c):
        n, rows, _ = c.shape
        c4 = c.reshape(n, rows, B_KV_HEADS, 1, B_HEAD_DIM)
        z = jnp.zeros_like(c4)
        lo = jnp.concatenate([c4, z], axis=-1)
        hi = jnp.concatenate([z, c4], axis=-1)
        return jnp.concatenate([lo, hi], axis=3).reshape(n, rows, 4 * kw).astype(BF16)

    heads = tuple(((t, 2 * (t // 2)), (t, 2 * (t // 2) + 1)) for t in range(qw // LANES))
    yp, kf, vf = _band_prompt(proj, xp, wo, reach=B_REACH, band_chunks=B_BAND_CHUNKS, heads=heads,
                              bias_base=bias_base, sinks=sinks2, name="b")
    n = xp.shape[0]
    head_shape = ck.shape[2:]

    ns, ts, d = xs.shape
    past = ck.shape[1]
    xs2 = xs.reshape(1, ns * ts, d)
    q, k, v, g, kf_s, vf_s = proj(xs2, None)
    ckf = ck.reshape(ns, past, -1)
    cvf = cv.reshape(ns, past, -1)
    cache_map = lambda b, i: (i, 0, 0)
    new_map = lambda b, i: (0, i, 0)
    bias_s = _band_bias(bias_base, ts, past + ts, past, None)
    h = _window_attn(q, [(expand_cache(ckf), past, cache_map), (k, ts, new_map)],
                     [(expand_cache(cvf), past, cache_map), (v, ts, new_map)], g, bias_s, sinks2,
                     grid=(1, ns), q_map=new_map, q_block=ts, heads=heads, need=(0, 0),
                     name="b_attn_sample")
    ys = _out_proj(xs2, h, wo).reshape(xs.shape)
    k_all = jnp.concatenate([ckf, kf_s.reshape(ns, ts, -1)], axis=1)[:, -B_REACH:]
    v_all = jnp.concatenate([cvf, vf_s.reshape(ns, ts, -1)], axis=1)[:, -B_REACH:]
    return (yp, ys, kf.reshape((n, B_REACH) + head_shape), vf.reshape((n, B_REACH) + head_shape),
            k_all.reshape((ns, B_REACH) + head_shape), v_all.reshape((ns, B_REACH) + head_shape))


def _rope_tables(pos):
    half = C_ROPE // 2
    inv = ROPE_THETA ** (-jnp.arange(half, dtype=F32) / half)
    ang = pos.astype(F32)[:, None] * inv[None, :]
    cos, sin = jnp.cos(ang), jnp.sin(ang)
    rows = pos.shape[0]
    ones = jnp.ones((rows, C_NOPE), F32)
    zeros = lambda width: jnp.zeros((rows, width), F32)
    pad = LANES - C_QK
    cos_t = jnp.concatenate([ones, cos, cos, zeros(pad)], axis=1)
    sin_t = jnp.concatenate([zeros(C_NOPE), -sin, sin, zeros(pad)], axis=1)
    return cos_t, sin_t


def _rope_padded(x):
    half = C_ROPE // 2
    zeros = jnp.zeros(x.shape[:-1] + (LANES - C_QK - half,), x.dtype)
    return jnp.concatenate([x, x[..., C_NOPE:C_NOPE + half], zeros], axis=-1)


def _proj_c(x, pos, gn, w, qag, wqb, kvag, qg, krg):
    n, s, d = x.shape
    tm = _row_tile(s)
    per_seq = pos.shape[0] // tm
    tables = _rope_tables(pos)
    table_spec = pl.BlockSpec((tm, LANES), lambda b, i: (i % per_seq, 0))
    consts = (gn, w, qag, wqb, kvag, qg, krg)
    return pl.pallas_call(
        functools.partial(_proj_c_kernel, scale=C_QK ** -0.5 * LOG2E),
        grid=(n, s // tm),
        in_specs=[_rows(tm, d)] + [_const_spec(c.shape) for c in consts] + [table_spec] * 2,
        out_specs=[_rows(tm, C_HEADS * LANES), _rows(tm, C_KV_LORA), _rows(tm, LANES),
                   _rows(tm, C_HEADS * C_V)],
        out_shape=[jax.ShapeDtypeStruct((n, s, C_HEADS * LANES), BF16),
                   jax.ShapeDtypeStruct((n, s, C_KV_LORA), F32),
                   jax.ShapeDtypeStruct((n, s, LANES), F32),
                   jax.ShapeDtypeStruct((n, s, C_HEADS * C_V), F32)],
        compiler_params=_params(2), name="proj_c")(x, *consts, *tables)


def _single(shape, index_map):
    return pl.BlockSpec(shape, index_map, pipeline_mode=pl.Buffered(1))


def _c_attn(q, ckv, kr, w, kg, g):
    n, s, _ = q.shape
    qb = min(ATTN_Q_BLOCK, s)
    kb = qb
    assert s % qb == 0 and qb % kb == 0 and kb % CHUNK == 0
    per_q = qb // kb
    row_chunk = jnp.arange(qb)[None, :, None] // CHUNK
    col_chunk = (jnp.arange(per_q)[:, None, None] * kb + jnp.arange(kb)[None, None, :]) // CHUNK
    mask = jnp.where(col_chunk <= row_chunk, 0.0, NEG_INF).astype(F32)
    width = C_HEADS * LANES
    whole = lambda w_: _single((1, s, w_), lambda b, i: (b, 0, 0))
    const = lambda a: _single(a.shape, lambda b, i: (0,) * a.ndim)
    return pl.pallas_call(
        functools.partial(_c_attn_kernel, kb=kb, expand_tile=qb), grid=(n, s // qb),
        in_specs=[_rows(qb, width), whole(C_KV_LORA), whole(LANES), const(w), const(kg),
                  _rows(qb, g.shape[-1]), const(mask)],
        out_specs=_rows(qb, g.shape[-1]),
        out_shape=jax.ShapeDtypeStruct(g.shape, BF16),
        scratch_shapes=[pltpu.VMEM((s, width), BF16), pltpu.VMEM((s, width), BF16),
                        pltpu.VMEM((C_HEADS, qb, LANES), F32), pltpu.VMEM((C_HEADS, qb, LANES), F32)],
        compiler_params=_params(2), name="c_attn_prompt")(q, ckv, kr, w, kg, g, mask)


def _c_attn_sample(q, cache_kv, cache_kr, first, new_kv, new_kr, w, kg, g, ns):
    past = cache_kv.shape[1]
    ts = q.shape[1] // ns
    width = C_HEADS * LANES
    tile = math.gcd(past, ATTN_Q_BLOCK)
    place = jnp.pad(jnp.eye(C_ROPE, dtype=BF16), ((0, 0), (C_NOPE, LANES - C_QK)))
    new = lambda w_: pl.BlockSpec((1, ts, w_), lambda b, i: (0, i, 0))
    cached = lambda w_: pl.BlockSpec((1, past, w_), lambda b, i: (first + i, 0, 0))
    return pl.pallas_call(
        functools.partial(_c_sample_kernel, expand_tile=tile), grid=(1, ns),
        in_specs=[new(width), cached(C_KV_LORA), cached(C_ROPE), new(C_KV_LORA), new(LANES),
                  _const_spec(w.shape), _const_spec(kg.shape), _const_spec(place.shape),
                  new(g.shape[-1])],
        out_specs=new(g.shape[-1]),
        out_shape=jax.ShapeDtypeStruct(g.shape, BF16),
        scratch_shapes=[pltpu.VMEM((past + ts, width), BF16), pltpu.VMEM((past + ts, width), BF16)],
        compiler_params=_params(2), name="c_attn_sample")(
            q, cache_kv, cache_kr, new_kv, new_kr, w, kg, place, g)


def _layer_c(xp, xs, cache_kv, cache_kr, layer, norm_g, w_in, qa_g, w_qb, kva_g, w_kvb, q_g, k_g,
             w_out):
    n, s, d = xp.shape
    o1 = C_Q_LORA
    o2 = o1 + C_KV_LORA
    o3 = o2 + C_ROPE
    kr_cols = _rope_padded(jnp.pad(w_in[:, o2:o3], ((0, 0), (C_NOPE, 0))))
    w = jnp.concatenate([w_in[:, :o2], w_in[:, o3:], kr_cols], axis=1).astype(BF16)
    wqb = _rope_padded(w_qb.reshape(C_Q_LORA, C_HEADS, C_QK))
    wqb = wqb.reshape(C_Q_LORA, C_HEADS * LANES).astype(BF16)
    kvb = w_kvb.reshape(C_KV_LORA, C_HEADS, C_NOPE + C_V)
    wk = jnp.pad(kvb[..., :C_NOPE], ((0, 0), (0, 0), (0, LANES - C_NOPE)))
    wv = kvb[..., C_NOPE:].reshape(C_KV_LORA, C_HEADS // 2, 2, C_V)
    zv = jnp.zeros_like(wv[:, :, 0])
    wv = jnp.stack([jnp.concatenate([wv[:, :, 0], zv], axis=-1),
                    jnp.concatenate([zv, wv[:, :, 1]], axis=-1)], axis=2)
    wkv = jnp.concatenate([wk.reshape(C_KV_LORA, -1), wv.reshape(C_KV_LORA, -1)], axis=1).astype(BF16)
    wo = w_out.astype(BF16)

    gn, qag, kvag = norm_g.reshape(1, -1), qa_g.reshape(1, -1), kva_g.reshape(1, -1)
    qg = _rope_padded(q_g.reshape(1, -1))
    krg = _rope_padded(jnp.pad(k_g[C_NOPE:].reshape(1, -1), ((0, 0), (C_NOPE, 0))))
    kg = jnp.pad(k_g[:C_NOPE].reshape(1, -1), ((0, 0), (0, LANES - C_NOPE)))

    q, ckv_p, kr_p, g = _proj_c(xp, jnp.arange(s), gn, w, qag, wqb, kvag, qg, krg)
    yp = _out_proj(xp, _c_attn(q, ckv_p, kr_p, wkv, kg, g), wo)

    n_layers, ns, past = cache_kv.shape[:3]
    ts = xs.shape[1]
    xs2 = xs.reshape(1, ns * ts, d)
    pos = past + jnp.arange(ts)
    q, ckv_s, kr_s, g = _proj_c(xs2, jnp.tile(pos, ns), gn, w, qag, wqb, kvag, qg, krg)
    h = _c_attn_sample(q, cache_kv.reshape(n_layers * ns, past, C_KV_LORA),
                       cache_kr.reshape(n_layers * ns, past, C_ROPE), layer * ns,
                       ckv_s, kr_s, wkv, kg, g, ns)
    ys = _out_proj(xs2, h, wo).reshape(xs.shape)
    rope = slice(C_NOPE, C_QK)
    return (yp, ys, ckv_p, kr_p[..., rope], ckv_s.reshape(ns, ts, C_KV_LORA),
            kr_s.reshape(ns, ts, LANES)[..., rope])


def kernel(x_prompt, x_sample, cache_a_k, cache_a_v, cache_b_k, cache_b_v, cache_c_kv, cache_c_kr,
           a_norm, a_w_in, a_q_norm, a_k_norm, a_rel_bias, a_w_out,
           b_norm, b_w_in, b_q_norm, b_k_norm, b_sinks, b_w_out,
           c_norm, c_w_in, c_q_a_norm, c_w_qb, c_kv_a_norm, c_w_kvb, c_q_norm, c_k_norm, c_w_out):
    depth = a_norm.shape[0] + b_norm.shape[0] + c_norm.shape[0]
    xp, xs = x_prompt, x_sample
    outs = [[] for _ in range(12)]
    a_k_all = a_v_all = None
    for layer in range(depth):
        j, kind = divmod(layer, N_MIXERS)
        if kind == 0:
            res = _layer_a(xp, xs, cache_a_k, cache_a_v, j, a_k_all, a_v_all, a_norm[j], a_w_in[j],
                           a_q_norm[j], a_k_norm[j], a_rel_bias[j], a_w_out[j])
            a_k_all, a_v_all = res[4], res[5]
        elif kind == 1:
            res = _layer_b(xp, xs, cache_b_k[j], cache_b_v[j], b_norm[j], b_w_in[j], b_q_norm[j],
                           b_k_norm[j], b_sinks[j], b_w_out[j])
        else:
            res = _layer_c(xp, xs, cache_c_kv, cache_c_kr, j, c_norm[j], c_w_in[j], c_q_a_norm[j],
                           c_w_qb[j], c_kv_a_norm[j], c_w_kvb[j], c_q_norm[j], c_k_norm[j],
                           c_w_out[j])
        xp, xs = res[0], res[1]
        for slot in range(4):
            outs[4 * kind + slot].append(res[2 + slot])
    outs[2], outs[3] = None, None
    stacked = [None if o is None else jnp.stack(o) for o in outs]
    stacked[2], stacked[3] = a_k_all, a_v_all
    order = [0, 1, 4, 5, 8, 9, 2, 3, 6, 7, 10, 11]
    return (xp, xs) + tuple(stacked[i] for i in order)
```

```python
import functools
import math

import jax
import jax.numpy as jnp
from jax import lax
from jax.experimental import pallas as pl
from jax.experimental.pallas import tpu as pltpu

F32 = jnp.float32
BF16 = jnp.bfloat16

LANES = 128
CHUNK = 64
NORM_EPS = 1e-6
NEG_INF = -1e30
LOG2E = math.log2(math.e)
N_MIXERS = 3

A_HEADS = 8
A_HEAD_DIM = 128
A_BAND_CHUNKS = 9
A_REL_CLIP = 128
A_REACH = (A_BAND_CHUNKS - 1) * CHUNK

B_HEADS = 16
B_KV_HEADS = 4
B_HEAD_DIM = 64
B_BAND_CHUNKS = 3
B_REACH = (B_BAND_CHUNKS - 1) * CHUNK

C_HEADS = 16
C_NOPE = 64
C_ROPE = 32
C_QK = C_NOPE + C_ROPE
C_V = 64
C_Q_LORA = 512
C_KV_LORA = 256
ROPE_THETA = 10000.0

ROW_TILE = 512
OUT_ROW_TILE = 1024
ATTN_Q_BLOCK = 256
VMEM_LIMIT = 56 * 1024 * 1024
C_ATTN_VMEM_LIMIT = 60 * 1024 * 1024


def _params(n_axes, vmem_limit=VMEM_LIMIT):
    return pltpu.CompilerParams(dimension_semantics=("arbitrary",) * n_axes,
                                vmem_limit_bytes=vmem_limit)


def _const_spec(shape):
    zeros = (0,) * len(shape)
    return pl.BlockSpec(shape, lambda *_: zeros)


def _rms(x, width):
    return x * lax.rsqrt(jnp.sum(x * x, axis=-1, keepdims=True) * (1.0 / width) + NORM_EPS)


def _normed_input(x_ref, gn_ref):
    x = x_ref[0]
    return (_rms(x, x.shape[-1]) * gn_ref[...]).astype(BF16)


def _tile(val, t):
    return val[:, t * LANES:(t + 1) * LANES]


def _silu(g):
    return g / (1.0 + jnp.exp(-g))


def _qk(q, k):
    return lax.dot_general(q, k, (((1,), (1,)), ((), ())), preferred_element_type=F32)


def _proj_a_kernel(x_ref, gn_ref, w_ref, qg_ref, kg_ref,
                   q_ref, k_ref, v_ref, g_ref, kf_ref, vf_ref, *, scale, tail):
    xn = _normed_input(x_ref, gn_ref)
    width = A_HEADS * A_HEAD_DIM
    rows = xn.shape[0]
    zq = jnp.dot(xn, w_ref[:, 0:width], preferred_element_type=F32)
    for h in range(A_HEADS):
        qn = _rms(_tile(zq, h), A_HEAD_DIM) * qg_ref[...]
        q_ref[0, :, h * LANES:(h + 1) * LANES] = (qn * scale).astype(BF16)
    by_head = kf_ref.ndim == 4
    zk = jnp.dot(xn, w_ref[:, width:2 * width], preferred_element_type=F32)
    k_tails = []
    for h in range(A_HEADS):
        kn = _rms(_tile(zk, h), A_HEAD_DIM) * kg_ref[...]
        k_ref[0, :, h * LANES:(h + 1) * LANES] = kn.astype(BF16)
        if by_head:
            k_tails.append(kn[rows - tail:, :])
        else:
            kf_ref[0, :, h * LANES:(h + 1) * LANES] = kn[rows - tail:, :]
    zv = jnp.dot(xn, w_ref[:, 2 * width:3 * width], preferred_element_type=F32)
    v_ref[0] = zv.astype(BF16)
    if by_head:
        @pl.when(pl.program_id(1) == pl.num_programs(1) - 1)
        def _write_tails():
            v_tails = [_tile(zv, h)[rows - tail:, :] for h in range(A_HEADS)]
            kf_ref[0] = jnp.swapaxes(jnp.stack(k_tails), 0, 1)
            vf_ref[0] = jnp.swapaxes(jnp.stack(v_tails), 0, 1)
    else:
        vf_ref[0] = zv[rows - tail:, :]
    g_ref[0] = jnp.dot(xn, w_ref[:, 3 * width:4 * width], preferred_element_type=F32)


def _half_rms(z, lo, width):
    sq = z * z
    s_lo = jnp.sum(jnp.where(lo, sq, 0.0), axis=-1, keepdims=True)
    s_hi = jnp.sum(jnp.where(lo, 0.0, sq), axis=-1, keepdims=True)
    r = jnp.where(lo, lax.rsqrt(s_lo * (1.0 / width) + NORM_EPS),
                  lax.rsqrt(s_hi * (1.0 / width) + NORM_EPS))
    return z * r


def _expand_halves(t, lo, out_ref, u):
    rolled = pltpu.roll(t, B_HEAD_DIM, 1)
    zero = jnp.zeros_like(t)
    tiles = (jnp.where(lo, t, zero), jnp.where(lo, zero, rolled),
             jnp.where(lo, rolled, zero), jnp.where(lo, zero, t))
    for i, val in enumerate(tiles):
        c = 4 * u + i
        out_ref[0, :, c * LANES:(c + 1) * LANES] = val.astype(BF16)


def _proj_b_kernel(x_ref, gn_ref, w_ref, qg_ref, kg_ref,
                   q_ref, k_ref, v_ref, g_ref, kf_ref, vf_ref, *, scale, tail):
    xn = _normed_input(x_ref, gn_ref)
    rows = xn.shape[0]
    qw = B_HEADS * B_HEAD_DIM
    kw = B_KV_HEADS * B_HEAD_DIM
    lo = lax.broadcasted_iota(jnp.int32, (rows, LANES), 1) < B_HEAD_DIM
    zq = jnp.dot(xn, w_ref[:, 0:qw], preferred_element_type=F32)
    for t in range(qw // LANES):
        qn = _half_rms(_tile(zq, t), lo, B_HEAD_DIM) * qg_ref[...]
        q_ref[0, :, t * LANES:(t + 1) * LANES] = (qn * scale).astype(BF16)
    zk = jnp.dot(xn, w_ref[:, qw:qw + kw], preferred_element_type=F32)
    for u in range(kw // LANES):
        kn = _half_rms(_tile(zk, u), lo, B_HEAD_DIM) * kg_ref[...]
        kf_ref[0, :, u * LANES:(u + 1) * LANES] = kn[rows - tail:, :]
        _expand_halves(kn, lo, k_ref, u)
    zv = jnp.dot(xn, w_ref[:, qw + kw:qw + 2 * kw], preferred_element_type=F32)
    vf_ref[0] = zv[rows - tail:, :]
    for u in range(kw // LANES):
        _expand_halves(_tile(zv, u), lo, v_ref, u)
    g_ref[0] = jnp.dot(xn, w_ref[:, qw + 2 * kw:2 * qw + 2 * kw], preferred_element_type=F32)


def _rotate(x, cos_ref, sin_ref):
    return x * cos_ref[...] + pltpu.roll(x, LANES - C_ROPE // 2, 1) * sin_ref[...]


def _proj_c_kernel(x_ref, gn_ref, w_ref, qag_ref, wqb_ref, kvag_ref, qg_ref, krg_ref,
                   cos_ref, sin_ref, q_ref, ckv_ref, kr_ref, g_ref, *, scale):
    xn = _normed_input(x_ref, gn_ref)
    rows = xn.shape[0]
    o1 = C_Q_LORA
    o2 = o1 + C_KV_LORA
    o3 = o2 + C_HEADS * C_V
    lane = lax.broadcasted_iota(jnp.int32, (rows, LANES), 1)
    nope = lane < C_NOPE
    rope = (lane >= C_NOPE) & (lane < C_QK)
    za = jnp.dot(xn, w_ref[:, 0:o1], preferred_element_type=F32)
    qa = (_rms(za, C_Q_LORA) * qag_ref[...]).astype(BF16)
    zq = jnp.dot(qa, wqb_ref[...], preferred_element_type=F32)
    for h in range(C_HEADS):
        z = _tile(zq, h)
        sq = z * z
        s_n = jnp.sum(jnp.where(nope, sq, 0.0), axis=-1, keepdims=True)
        s_r = jnp.sum(jnp.where(rope, sq, 0.0), axis=-1, keepdims=True)
        r = jnp.where(nope, lax.rsqrt(s_n * (1.0 / C_NOPE) + NORM_EPS),
                      lax.rsqrt(s_r * (1.0 / C_ROPE) + NORM_EPS))
        qn = _rotate(z * r * qg_ref[...], cos_ref, sin_ref)
        q_ref[0, :, h * LANES:(h + 1) * LANES] = (qn * scale).astype(BF16)
    zc = jnp.dot(xn, w_ref[:, o1:o2], preferred_element_type=F32)
    ckv_ref[0] = _rms(zc, C_KV_LORA) * kvag_ref[...]
    g_ref[0] = jnp.dot(xn, w_ref[:, o2:o3], preferred_element_type=F32)
    zr = jnp.dot(xn, w_ref[:, o3:o3 + LANES], preferred_element_type=F32)
    s_k = jnp.sum(jnp.where(rope, zr * zr, 0.0), axis=-1, keepdims=True)
    krn = zr * lax.rsqrt(s_k * (1.0 / C_ROPE) + NORM_EPS) * krg_ref[...]
    kr_ref[0] = _rotate(krn, cos_ref, sin_ref)


def _expand_rows(ckv, kr, w_ref, kg_ref, k_scr, v_scr, rows_at):
    ckv = ckv.astype(BF16)
    width = C_HEADS * LANES
    lane = lax.broadcasted_iota(jnp.int32, (ckv.shape[0], LANES), 1)
    zk = jnp.dot(ckv, w_ref[:, 0:width], preferred_element_type=F32)
    for h in range(C_HEADS):
        kn = _rms(_tile(zk, h), C_NOPE) * kg_ref[...] + kr
        k_scr[rows_at, h * LANES:(h + 1) * LANES] = kn.astype(BF16)
    zv = jnp.dot(ckv, w_ref[:, width:2 * width], preferred_element_type=F32)
    for h in range(C_HEADS):
        ones_lane = C_V if h % 2 == 0 else 0
        v_scr[rows_at, h * LANES:(h + 1) * LANES] = jnp.where(lane == ones_lane, 1.0,
                                                              _tile(zv, h)).astype(BF16)


def _out_proj_kernel(x_ref, h_ref, w_ref, y_ref):
    y_ref[0] = x_ref[0] + jnp.dot(h_ref[0], w_ref[...], preferred_element_type=F32)


def _row_reduce(pieces, combine, reduce):
    tiles, narrow = [], []
    for piece in pieces:
        width = piece.shape[1]
        if width % LANES == 0:
            tiles += [piece[:, i * LANES:(i + 1) * LANES] for i in range(width // LANES)]
        else:
            narrow.append(piece)
    folded = ([functools.reduce(combine, tiles)] if tiles else []) + narrow
    return functools.reduce(combine, [reduce(x, axis=-1, keepdims=True) for x in folded])


def _softmax_parts(scores, sink):
    m = _row_reduce(scores, jnp.maximum, jnp.max)
    if sink is not None:
        m = jnp.maximum(m, sink)
    probs = [jnp.exp2(s - m) for s in scores]
    denom = _row_reduce(probs, jnp.add, jnp.sum)
    if sink is not None:
        denom = denom + jnp.exp2(sink - m)
    return probs, denom


def _window_attn_kernel(*refs, n_pieces, heads, need, has_bias, has_sinks):
    q_ref = refs[0]
    k_refs = refs[1:1 + n_pieces]
    v_refs = refs[1 + n_pieces:1 + 2 * n_pieces]
    pos = 1 + 2 * n_pieces
    g_ref = refs[pos]
    pos += 1
    bias_ref = None
    if has_bias:
        bias_ref = refs[pos]
        pos += 1
    sink_ref = None
    if has_sinks:
        sink_ref = refs[pos]
        pos += 1
    h_ref = refs[pos]

    step = pl.program_id(1)
    rows = q_ref.shape[1]
    lo = lax.broadcasted_iota(jnp.int32, (rows, LANES), 1) < LANES // 2
    pen = [jnp.where(step >= nd, 0.0, NEG_INF).astype(F32) if nd > 0 else None for nd in need]
    for t in range(len(heads)):
        acc = None
        for sub, (q_t, kv_t) in enumerate(heads[t]):
            head = t * len(heads[t]) + sub
            qt = q_ref[0, :, q_t * LANES:(q_t + 1) * LANES]
            scores = []
            off = 0
            for j in range(n_pieces):
                kj = k_refs[j][0, :, kv_t * LANES:(kv_t + 1) * LANES]
                s = _qk(qt, kj)
                if has_bias:
                    s = s + bias_ref[head, :, off:off + kj.shape[0]]
                if pen[j] is not None:
                    s = s + pen[j]
                scores.append(s)
                off += kj.shape[0]
            probs, denom = _softmax_parts(scores, sink_ref[head] if has_sinks else None)
            o = None
            for j in range(n_pieces):
                vj = v_refs[j][0, :, kv_t * LANES:(kv_t + 1) * LANES]
                pv = jnp.dot(probs[j].astype(BF16), vj, preferred_element_type=F32)
                o = pv if o is None else o + pv
            o = o / denom
            acc = o if acc is None else jnp.where(lo, acc, o)
        g = g_ref[0, :, t * LANES:(t + 1) * LANES]
        h_ref[0, :, t * LANES:(t + 1) * LANES] = (acc * _silu(g)).astype(BF16)


def _a_sample_kernel(q_ref, kn_ref, vn_ref, knf_ref, vnf_ref, ck_ref, cv_ref, g_ref, bias_ref,
                     *rest, slot):
    h_ref, ok_ref, ov_ref = rest[-3:]
    past = ck_ref.shape[2]
    new = q_ref.shape[1]
    reach = ok_ref.shape[2]
    drop = past + new - reach
    for other in range(ok_ref.shape[0]):
        if other != slot:
            ok_ref[other] = jnp.zeros(ok_ref.shape[1:], F32)
            ov_ref[other] = jnp.zeros(ov_ref.shape[1:], F32)
    k_heads = jnp.swapaxes(ck_ref[0, 0], 0, 1)
    v_heads = jnp.swapaxes(cv_ref[0, 0], 0, 1)
    k_out, v_out = [], []
    for h in range(A_HEADS):
        cols = slice(h * LANES, (h + 1) * LANES)
        kc = k_heads[h]
        vc = v_heads[h]
        qh = q_ref[0, :, cols]
        s_c = _qk(qh, kc.astype(BF16)) + bias_ref[h, :, 0:past]
        s_n = _qk(qh, kn_ref[0, :, cols]) + bias_ref[h, :, past:past + new]
        (p_c, p_n), denom = _softmax_parts([s_c, s_n], None)
        o = (jnp.dot(p_c.astype(BF16), vc.astype(BF16), preferred_element_type=F32)
             + jnp.dot(p_n.astype(BF16), vn_ref[0, :, cols], preferred_element_type=F32)) / denom
        h_ref[0, :, cols] = (o * _silu(g_ref[0, :, cols])).astype(BF16)
        k_out.append(jnp.concatenate([kc[drop:, :], knf_ref[0, :, cols]], axis=0))
        v_out.append(jnp.concatenate([vc[drop:, :], vnf_ref[0, :, cols]], axis=0))
    ok_ref[slot, 0] = jnp.swapaxes(jnp.stack(k_out), 0, 1)
    ov_ref[slot, 0] = jnp.swapaxes(jnp.stack(v_out), 0, 1)


def _online_update(c, q, kj, vj, bias, m_scr, acc_scr):
    s = _qk(q, kj)
    if bias is not None:
        s = s + bias
    parts = [s[:, i * LANES:(i + 1) * LANES] for i in range(s.shape[1] // LANES)]
    m_old = m_scr[c]
    m_new = jnp.maximum(m_old, jnp.max(functools.reduce(jnp.maximum, parts), axis=-1,
                                       keepdims=True))
    m_scr[c] = m_new
    p = jnp.concatenate([jnp.exp2(x - m_new).astype(BF16) for x in parts], axis=1)
    acc_scr[c] = (acc_scr[c] * jnp.exp2(m_old - m_new)
                  + jnp.dot(p, vj, preferred_element_type=F32))


def _band_attn_kernel(*refs, n_pieces, heads, need, has_sinks):
    q_ref = refs[0]
    k_refs = refs[1:1 + n_pieces]
    v_refs = refs[1 + n_pieces:1 + 2 * n_pieces]
    g_ref, bias_ref = refs[1 + 2 * n_pieces:3 + 2 * n_pieces]
    pos = 3 + 2 * n_pieces
    sink_ref = None
    if has_sinks:
        sink_ref = refs[pos]
        pos += 1
    x_ref, wo_ref, y_ref, s_scr, m_scr, h_scr = refs[pos:pos + 6]

    step = pl.program_id(1)
    rows = q_ref.shape[1]
    lo = lax.broadcasted_iota(jnp.int32, (rows, LANES), 1) < LANES // 2
    flat = [(t, sub, q_t, kv_t) for t in range(len(heads)) for sub, (q_t, kv_t) in enumerate(heads[t])]
    per_tile = len(heads[0])

    offsets = [sum(k_refs[i].shape[1] for i in range(j)) for j in range(n_pieces)]

    for j in range(n_pieces):
        width = k_refs[j].shape[1]
        pen = jnp.where(step >= need[j], 0.0, NEG_INF).astype(F32) if need[j] > 0 else None
        for t, sub, q_t, kv_t in flat:
            c = t * per_tile + sub
            s = _qk(q_ref[0, :, q_t * LANES:(q_t + 1) * LANES],
                    k_refs[j][0, :, kv_t * LANES:(kv_t + 1) * LANES])
            s = s + bias_ref[c, :, offsets[j]:offsets[j] + width]
            if pen is not None:
                s = s + pen
            s_scr[c, :, offsets[j]:offsets[j] + width] = s
            tiles = [s[:, i * LANES:(i + 1) * LANES] for i in range(width // LANES)]
            if j > 0:
                tiles.append(m_scr[c])
            m_scr[c] = functools.reduce(jnp.maximum, tiles)

    slot0 = jnp.minimum(step, 0)
    for t in range(len(heads)):
        out = None
        for sub, (_, kv_t) in enumerate(heads[t]):
            c = t * per_tile + sub
            m = jnp.max(m_scr[c + slot0], axis=-1, keepdims=True)
            if has_sinks:
                m = jnp.maximum(m, sink_ref[c])
            lanes_sum = None
            o = None
            for j in range(n_pieces):
                width = k_refs[j].shape[1]
                p = jnp.exp2(s_scr[c + slot0, :, offsets[j]:offsets[j] + width] - m)
                for i in range(width // LANES):
                    tile = p[:, i * LANES:(i + 1) * LANES]
                    lanes_sum = tile if lanes_sum is None else lanes_sum + tile
                pv = jnp.dot(p.astype(BF16), v_refs[j][0, :, kv_t * LANES:(kv_t + 1) * LANES],
                             preferred_element_type=F32)
                o = pv if o is None else o + pv
            denom = jnp.sum(lanes_sum, axis=-1, keepdims=True)
            if has_sinks:
                denom = denom + jnp.exp2(sink_ref[c] - m)
            o = o / denom
            out = o if out is None else jnp.where(lo, out, o)
        g = g_ref[0, :, t * LANES:(t + 1) * LANES]
        h_scr[:, t * LANES:(t + 1) * LANES] = (out * _silu(g)).astype(BF16)

    y_ref[0] = x_ref[0] + jnp.dot(h_scr[...], wo_ref[...], preferred_element_type=F32)


def _c_attn_kernel(q_ref, ckv_ref, kr_ref, w_ref, kg_ref, g_ref, mask_ref, x_ref, wo_ref, y_ref,
                   k_scr, v_scr, m_scr, acc_scr, h_scr, *, kb, expand_tile):
    step = pl.program_id(1)
    rows = q_ref.shape[1]
    lane = lax.broadcasted_iota(jnp.int32, (rows, LANES), 1)
    lo = lane < C_V
    per_q = rows // kb

    @pl.when(step == 0)
    def _expand_sequence():
        def expand(r, carry):
            at = pl.ds(pl.multiple_of(r * expand_tile, expand_tile), expand_tile)
            _expand_rows(ckv_ref[0, at, :], kr_ref[0, at, :], w_ref, kg_ref, k_scr, v_scr, at)
            return carry
        lax.fori_loop(0, ckv_ref.shape[1] // expand_tile, expand, 0)

    m_scr[...] = jnp.full(m_scr.shape, NEG_INF, F32)
    acc_scr[...] = jnp.zeros(acc_scr.shape, F32)

    def block(j, masked):
        start = pl.multiple_of(j * kb, kb)
        for c in range(C_HEADS):
            cols = slice(c * LANES, (c + 1) * LANES)
            _online_update(c, q_ref[0, :, cols], k_scr[pl.ds(start, kb), cols],
                           v_scr[pl.ds(start, kb), cols],
                           None if masked is None else mask_ref[masked], m_scr, acc_scr)

    def body(j, carry):
        block(j, None)
        return carry

    lax.fori_loop(0, step * per_q, body, 0)
    for d in range(per_q):
        block(step * per_q + d, d)

    for t in range(C_HEADS // 2):
        a_e = acc_scr[2 * t]
        a_o = acc_scr[2 * t + 1]
        l_e = jnp.sum(jnp.where(lane == C_V, a_e, 0.0), axis=-1, keepdims=True)
        l_o = jnp.sum(jnp.where(lane == 0, a_o, 0.0), axis=-1, keepdims=True)
        g = g_ref[0, :, t * LANES:(t + 1) * LANES]
        o = jnp.where(lo, a_e / l_e, a_o / l_o)
        h_scr[:, t * LANES:(t + 1) * LANES] = (o * _silu(g)).astype(BF16)

    y_ref[0] = x_ref[0] + jnp.dot(h_scr[...], wo_ref[...], preferred_element_type=F32)


def _c_sample_kernel(q_ref, ckv_ref, ckr_ref, nkv_ref, nkr_ref, w_ref, kg_ref, place_ref, g_ref,
                     h_ref, k_scr, v_scr, *, expand_tile):
    past = ckv_ref.shape[1]
    new = q_ref.shape[1]
    lo = lax.broadcasted_iota(jnp.int32, (new, LANES), 1) < C_V

    def expand(r, carry):
        at = pl.ds(pl.multiple_of(r * expand_tile, expand_tile), expand_tile)
        kr = jnp.dot(ckr_ref[0, at, :].astype(BF16), place_ref[...], preferred_element_type=F32)
        _expand_rows(ckv_ref[0, at, :], kr, w_ref, kg_ref, k_scr, v_scr, at)
        return carry
    lax.fori_loop(0, past // expand_tile, expand, 0)
    _expand_rows(nkv_ref[0], nkr_ref[0], w_ref, kg_ref, k_scr, v_scr, pl.ds(past, new))

    for t in range(C_HEADS // 2):
        acc = None
        for c in (2 * t, 2 * t + 1):
            cols = slice(c * LANES, (c + 1) * LANES)
            q = q_ref[0, :, cols]
            s_c = _qk(q, k_scr[0:past, cols])
            s_n = _qk(q, k_scr[past:past + new, cols])
            (p_c, p_n), denom = _softmax_parts([s_c, s_n], None)
            o = (jnp.dot(p_c.astype(BF16), v_scr[0:past, cols], preferred_element_type=F32)
                 + jnp.dot(p_n.astype(BF16), v_scr[past:past + new, cols],
                           preferred_element_type=F32)) / denom
            acc = o if acc is None else jnp.where(lo, acc, o)
        g = g_ref[0, :, t * LANES:(t + 1) * LANES]
        h_ref[0, :, t * LANES:(t + 1) * LANES] = (acc * _silu(g)).astype(BF16)


def _rows(tm, width):
    return pl.BlockSpec((1, tm, width), lambda b, i: (b, i, 0))


def _cache_out(n, s, tm, width, tail, head_shape):
    if tail is None:
        return jax.ShapeDtypeStruct((n, s, width), F32), _rows(tm, width), tm
    assert tail <= tm and s % tm == 0
    minor = (width,) if head_shape is None else tuple(head_shape)
    zeros = (0,) * (1 + len(minor))
    spec = pl.BlockSpec((1, tail) + minor, lambda b, i: (b,) + zeros)
    return jax.ShapeDtypeStruct((n, tail) + minor, F32), spec, tail


def _row_tile(s):
    tm = min(ROW_TILE, s)
    assert s % tm == 0
    return tm


def _proj_ab(kernel, x, gn, w, qg, kg, *, q_width, kv_width, g_width, scale, tail, name,
             tail_head_shape=None):
    n, s, d = x.shape
    tm = _row_tile(s)
    cache_width = (w.shape[1] - q_width - g_width) // 2
    cache_shape, cache_spec, tail_rows = _cache_out(n, s, tm, cache_width, tail, tail_head_shape)
    return pl.pallas_call(
        functools.partial(kernel, scale=scale, tail=tail_rows),
        grid=(n, s // tm),
        in_specs=[_rows(tm, d), _const_spec(gn.shape), _const_spec(w.shape),
                  _const_spec(qg.shape), _const_spec(kg.shape)],
        out_specs=[_rows(tm, q_width), _rows(tm, kv_width), _rows(tm, kv_width), _rows(tm, g_width),
                   cache_spec, cache_spec],
        out_shape=[jax.ShapeDtypeStruct((n, s, q_width), BF16),
                   jax.ShapeDtypeStruct((n, s, kv_width), BF16),
                   jax.ShapeDtypeStruct((n, s, kv_width), BF16),
                   jax.ShapeDtypeStruct((n, s, g_width), F32), cache_shape, cache_shape],
        compiler_params=_params(2), name=name)(x, gn, w, qg, kg)


def _out_proj(x, h, w):
    n, s, d = x.shape
    tm = math.gcd(s, OUT_ROW_TILE)
    return pl.pallas_call(
        _out_proj_kernel, grid=(n, s // tm),
        in_specs=[_rows(tm, d), _rows(tm, h.shape[-1]), _const_spec(w.shape)],
        out_specs=_rows(tm, d), out_shape=jax.ShapeDtypeStruct((n, s, d), F32),
        compiler_params=_params(2), name="out_proj")(x, h, w)


def _window_attn(q, k_pieces, v_pieces, g, bias, sinks, *, grid, q_map, q_block, heads, need, name):
    width = len(heads) * LANES
    in_specs = [pl.BlockSpec((1, q_block, q.shape[-1]), q_map)]
    args = [q]
    for arr, rows, imap in list(k_pieces) + list(v_pieces):
        in_specs.append(pl.BlockSpec((1, rows, arr.shape[-1]), imap))
        args.append(arr)
    in_specs.append(pl.BlockSpec((1, q_block, width), q_map))
    args.append(g)
    if bias is not None:
        in_specs.append(_const_spec(bias.shape))
        args.append(bias)
    if sinks is not None:
        in_specs.append(pl.BlockSpec(memory_space=pltpu.SMEM))
        args.append(sinks)
    kernel = functools.partial(_window_attn_kernel, n_pieces=len(k_pieces), heads=heads, need=need,
                               has_bias=bias is not None, has_sinks=sinks is not None)
    return pl.pallas_call(
        kernel, grid=grid, in_specs=in_specs,
        out_specs=pl.BlockSpec((1, q_block, width), q_map),
        out_shape=jax.ShapeDtypeStruct(g.shape[:2] + (width,), BF16),
        compiler_params=_params(2), name=name)(*args)


def _band_bias(base, q_rows, k_rows, q_off, band_chunks):
    m = q_rows + k_rows - 1
    vec = base(q_off + (q_rows - 1) - jnp.arange(m)) * LOG2E
    flat = jnp.tile(vec, (1, q_rows))[:, q_rows - 1:q_rows - 1 + q_rows * (m - 1)]
    bias = flat.reshape(-1, q_rows, m - 1)[:, :, :k_rows]
    if band_chunks is None:
        return bias
    i = jnp.arange(q_rows)[:, None]
    j = jnp.arange(k_rows)[None, :]
    lag = (q_off + i) // CHUNK - j // CHUNK
    return jnp.where((lag >= 0) & (lag < band_chunks), bias, NEG_INF)


def _band_prompt_pieces(arr, q_block, reach):
    pieces, need = [], []
    if reach % q_block == 0:
        for r in range(reach // q_block, 0, -1):
            pieces.append((arr, q_block, lambda b, i, r=r: (b, jnp.maximum(i - r, 0), 0)))
            need.append(r)
    else:
        assert q_block % reach == 0
        ratio = q_block // reach
        pieces.append((arr, reach, lambda b, i: (b, jnp.maximum(i * ratio - 1, 0), 0)))
        need.append(1)
    pieces.append((arr, q_block, lambda b, i: (b, i, 0)))
    need.append(0)
    return pieces, tuple(need)


def _band_prompt(proj, xp, wo, *, reach, band_chunks, heads, bias_base, sinks, name):
    n, s, d = xp.shape
    qb = min(ATTN_Q_BLOCK, s)
    assert s % qb == 0 and s >= reach
    q, k, v, g, kf, vf = proj(xp, reach)
    kp, need = _band_prompt_pieces(k, qb, reach)
    vp, _ = _band_prompt_pieces(v, qb, reach)
    bias = _band_bias(bias_base, qb, reach + qb, reach, band_chunks)
    width = len(heads) * LANES
    n_slots = sum(len(hs) for hs in heads)
    pieces = kp + vp
    args = [q] + [arr for arr, _, _ in pieces] + [g, bias]
    in_specs = ([_rows(qb, q.shape[-1])]
                + [pl.BlockSpec((1, rows, arr.shape[-1]), imap) for arr, rows, imap in pieces]
                + [_rows(qb, width), _single(bias.shape, lambda b, i: (0, 0, 0))])
    if sinks is not None:
        in_specs.append(pl.BlockSpec(memory_space=pltpu.SMEM))
        args.append(sinks)
    in_specs += [_rows(qb, d), _single(wo.shape, lambda b, i: (0, 0))]
    args += [xp, wo]
    scratch = [pltpu.VMEM((n_slots, qb, reach + qb), F32), pltpu.VMEM((n_slots, qb, LANES), F32),
               pltpu.VMEM((qb, width), BF16)]
    yp = pl.pallas_call(
        functools.partial(_band_attn_kernel, n_pieces=len(kp), heads=heads, need=need,
                          has_sinks=sinks is not None),
        grid=(n, s // qb), in_specs=in_specs, out_specs=_rows(qb, d),
        out_shape=jax.ShapeDtypeStruct((n, s, d), F32),
        scratch_shapes=scratch,
        compiler_params=_params(2), name=name + "_attn_prompt")(*args)
    return yp, kf, vf


def _layer_a(xp, xs, cache_k, cache_v, layer, prev_k, prev_v, norm_g, w_in, q_g, k_g, rel, w_out):
    width = A_HEADS * A_HEAD_DIM
    w = w_in.astype(BF16)
    wo = w_out.astype(BF16)
    gn, qg, kg = norm_g.reshape(1, -1), q_g.reshape(1, -1), k_g.reshape(1, -1)

    head_shape = (A_HEADS, A_HEAD_DIM)

    def proj(x, tail):
        return _proj_ab(_proj_a_kernel, x, gn, w, qg, kg, q_width=width, kv_width=width,
                        g_width=width, scale=A_HEAD_DIM ** -0.5 * LOG2E, tail=tail, name="proj_a",
                        tail_head_shape=head_shape)

    def bias_base(dist):
        idx = jnp.clip(dist, -A_REL_CLIP, A_REL_CLIP) + A_REL_CLIP
        return rel.astype(F32)[:, idx]

    heads = tuple(((t, t),) for t in range(A_HEADS))
    yp, kf, vf = _band_prompt(proj, xp, wo, reach=A_REACH, band_chunks=A_BAND_CHUNKS, heads=heads,
                              bias_base=bias_base, sinks=None, name="a")

    n_layers, ns, past = cache_k.shape[:3]
    ts, d = xs.shape[1:]
    xs2 = xs.reshape(1, ns * ts, d)
    q, k, v, g, knf, vnf = proj(xs2, None)
    bias_s = _band_bias(bias_base, ts, past + ts, past, None)
    new = lambda w_: pl.BlockSpec((1, ts, w_), lambda b, i: (0, i, 0))
    cache_block = lambda rows: pl.BlockSpec((1, 1, rows) + head_shape,
                                            lambda b, i: (layer, i, 0, 0, 0))
    out_cache = jax.ShapeDtypeStruct((n_layers, ns, A_REACH) + head_shape, F32)
    prev = [] if prev_k is None else [prev_k, prev_v]
    if prev:
        out_block, slot = cache_block(A_REACH), 0
    else:
        out_block = pl.BlockSpec((n_layers, 1, A_REACH) + head_shape, lambda b, i: (0, i, 0, 0, 0))
        slot = layer
    n_in = 9
    h, k_all, v_all = pl.pallas_call(
        functools.partial(_a_sample_kernel, slot=slot), grid=(1, ns),
        in_specs=[new(width)] * 5 + [cache_block(past)] * 2 + [new(width), _const_spec(bias_s.shape)]
        + [pl.BlockSpec(memory_space=pl.ANY)] * len(prev),
        out_specs=[new(width), out_block, out_block],
        out_shape=[jax.ShapeDtypeStruct((1, ns * ts, width), BF16), out_cache, out_cache],
        input_output_aliases={n_in + i: 1 + i for i in range(len(prev))},
        compiler_params=_params(2), name="a_attn_sample")(
            q, k, v, knf, vnf, cache_k, cache_v, g, bias_s, *prev)
    ys = _out_proj(xs2, h, wo).reshape(xs.shape)
    return yp, ys, kf, vf, k_all, v_all


def _layer_b(xp, xs, ck, cv, norm_g, w_in, q_g, k_g, sinks, w_out):
    qw = B_HEADS * B_HEAD_DIM
    kw = B_KV_HEADS * B_HEAD_DIM
    w = w_in.astype(BF16)
    wo = w_out.astype(BF16)
    gn = norm_g.reshape(1, -1)
    qg = jnp.tile(q_g.reshape(1, -1), (1, 2))
    kg = jnp.tile(k_g.reshape(1, -1), (1, 2))
    sinks2 = sinks.astype(F32) * LOG2E

    def proj(x, tail):
        return _proj_ab(_proj_b_kernel, x, gn, w, qg, kg, q_width=qw, kv_width=4 * kw, g_width=qw,
                        scale=B_HEAD_DIM ** -0.5 * LOG2E, tail=tail, name="proj_b")

    slopes = 2.0 ** (-8.0 * jnp.arange(1, B_HEADS + 1, dtype=F32) / B_HEADS)

    def bias_base(dist):
        return -slopes[:, None] * jnp.abs(dist).astype(F32)[None, :]

    def expand_cache(c):
        n, rows, _ = c.shape
        c4 = c.reshape(n, rows, B_KV_HEADS, 1, B_HEAD_DIM)
        z = jnp.zeros_like(c4)
        lo = jnp.concatenate([c4, z], axis=-1)
        hi = jnp.concatenate([z, c4], axis=-1)
        return jnp.concatenate([lo, hi], axis=3).reshape(n, rows, 4 * kw).astype(BF16)

    heads = tuple(((t, 2 * (t // 2)), (t, 2 * (t // 2) + 1)) for t in range(qw // LANES))
    yp, kf, vf = _band_prompt(proj, xp, wo, reach=B_REACH, band_chunks=B_BAND_CHUNKS, heads=heads,
                              bias_base=bias_base, sinks=sinks2, name="b")
    n = xp.shape[0]
    head_shape = ck.shape[2:]

    ns, ts, d = xs.shape
    past = ck.shape[1]
    xs2 = xs.reshape(1, ns * ts, d)
    q, k, v, g, kf_s, vf_s = proj(xs2, None)
    ckf = ck.reshape(ns, past, -1)
    cvf = cv.reshape(ns, past, -1)
    cache_map = lambda b, i: (i, 0, 0)
    new_map = lambda b, i: (0, i, 0)
    bias_s = _band_bias(bias_base, ts, past + ts, past, None)
    h = _window_attn(q, [(expand_cache(ckf), past, cache_map), (k, ts, new_map)],
                     [(expand_cache(cvf), past, cache_map), (v, ts, new_map)], g, bias_s, sinks2,
                     grid=(1, ns), q_map=new_map, q_block=ts, heads=heads, need=(0, 0),
                     name="b_attn_sample")
    ys = _out_proj(xs2, h, wo).reshape(xs.shape)
    k_all = jnp.concatenate([ckf, kf_s.reshape(ns, ts, -1)], axis=1)[:, -B_REACH:]
    v_all = jnp.concatenate([cvf, vf_s.reshape(ns, ts, -1)], axis=1)[:, -B_REACH:]
    return (yp, ys, kf.reshape((n, B_REACH) + head_shape), vf.reshape((n, B_REACH) + head_shape),
            k_all.reshape((ns, B_REACH) + head_shape), v_all.reshape((ns, B_REACH) + head_shape))


def _rope_tables(pos):
    half = C_ROPE // 2
    inv = ROPE_THETA ** (-jnp.arange(half, dtype=F32) / half)
    ang = pos.astype(F32)[:, None] * inv[None, :]
    cos, sin = jnp.cos(ang), jnp.sin(ang)
    rows = pos.shape[0]
    ones = jnp.ones((rows, C_NOPE), F32)
    zeros = lambda width: jnp.zeros((rows, width), F32)
    pad = LANES - C_QK
    cos_t = jnp.concatenate([ones, cos, cos, zeros(pad)], axis=1)
    sin_t = jnp.concatenate([zeros(C_NOPE), -sin, sin, zeros(pad)], axis=1)
    return cos_t, sin_t


def _rope_padded(x):
    half = C_ROPE // 2
    zeros = jnp.zeros(x.shape[:-1] + (LANES - C_QK - half,), x.dtype)
    return jnp.concatenate([x, x[..., C_NOPE:C_NOPE + half], zeros], axis=-1)


def _proj_c(x, pos, gn, w, qag, wqb, kvag, qg, krg):
    n, s, d = x.shape
    tm = _row_tile(s)
    per_seq = pos.shape[0] // tm
    tables = _rope_tables(pos)
    table_spec = pl.BlockSpec((tm, LANES), lambda b, i: (i % per_seq, 0))
    consts = (gn, w, qag, wqb, kvag, qg, krg)
    return pl.pallas_call(
        functools.partial(_proj_c_kernel, scale=C_QK ** -0.5 * LOG2E),
        grid=(n, s // tm),
        in_specs=[_rows(tm, d)] + [_const_spec(c.shape) for c in consts] + [table_spec] * 2,
        out_specs=[_rows(tm, C_HEADS * LANES), _rows(tm, C_KV_LORA), _rows(tm, LANES),
                   _rows(tm, C_HEADS * C_V)],
        out_shape=[jax.ShapeDtypeStruct((n, s, C_HEADS * LANES), BF16),
                   jax.ShapeDtypeStruct((n, s, C_KV_LORA), F32),
                   jax.ShapeDtypeStruct((n, s, LANES), F32),
                   jax.ShapeDtypeStruct((n, s, C_HEADS * C_V), F32)],
        compiler_params=_params(2), name="proj_c")(x, *consts, *tables)


def _single(shape, index_map):
    return pl.BlockSpec(shape, index_map, pipeline_mode=pl.Buffered(1))


def _c_attn(q, ckv, kr, w, kg, g, x, wo):
    n, s, _ = q.shape
    qb = min(ATTN_Q_BLOCK, s)
    kb = qb
    assert s % qb == 0 and qb % kb == 0 and kb % CHUNK == 0
    per_q = qb // kb
    row_chunk = jnp.arange(qb)[None, :, None] // CHUNK
    col_chunk = (jnp.arange(per_q)[:, None, None] * kb + jnp.arange(kb)[None, None, :]) // CHUNK
    mask = jnp.where(col_chunk <= row_chunk, 0.0, NEG_INF).astype(F32)
    width = C_HEADS * LANES
    whole = lambda w_: _single((1, s, w_), lambda b, i: (b, 0, 0))
    const = lambda a: _single(a.shape, lambda b, i: (0,) * a.ndim)
    d = x.shape[-1]
    return pl.pallas_call(
        functools.partial(_c_attn_kernel, kb=kb, expand_tile=qb), grid=(n, s // qb),
        in_specs=[_rows(qb, width), whole(C_KV_LORA), whole(LANES), const(w), const(kg),
                  _rows(qb, g.shape[-1]), const(mask), _rows(qb, d), const(wo)],
        out_specs=_rows(qb, d),
        out_shape=jax.ShapeDtypeStruct(x.shape, F32),
        scratch_shapes=[pltpu.VMEM((s, width), BF16), pltpu.VMEM((s, width), BF16),
                        pltpu.VMEM((C_HEADS, qb, LANES), F32), pltpu.VMEM((C_HEADS, qb, LANES), F32),
                        pltpu.VMEM((qb, g.shape[-1]), BF16)],
        compiler_params=_params(2, C_ATTN_VMEM_LIMIT), name="c_attn_prompt")(
            q, ckv, kr, w, kg, g, mask, x, wo)


def _c_attn_sample(q, cache_kv, cache_kr, first, new_kv, new_kr, w, kg, g, ns):
    past = cache_kv.shape[1]
    ts = q.shape[1] // ns
    width = C_HEADS * LANES
    tile = math.gcd(past, ATTN_Q_BLOCK)
    place = jnp.pad(jnp.eye(C_ROPE, dtype=BF16), ((0, 0), (C_NOPE, LANES - C_QK)))
    new = lambda w_: pl.BlockSpec((1, ts, w_), lambda b, i: (0, i, 0))
    cached = lambda w_: pl.BlockSpec((1, past, w_), lambda b, i: (first + i, 0, 0))
    return pl.pallas_call(
        functools.partial(_c_sample_kernel, expand_tile=tile), grid=(1, ns),
        in_specs=[new(width), cached(C_KV_LORA), cached(C_ROPE), new(C_KV_LORA), new(LANES),
                  _const_spec(w.shape), _const_spec(kg.shape), _const_spec(place.shape),
                  new(g.shape[-1])],
        out_specs=new(g.shape[-1]),
        out_shape=jax.ShapeDtypeStruct(g.shape, BF16),
        scratch_shapes=[pltpu.VMEM((past + ts, width), BF16), pltpu.VMEM((past + ts, width), BF16)],
        compiler_params=_params(2), name="c_attn_sample")(
            q, cache_kv, cache_kr, new_kv, new_kr, w, kg, place, g)


def _layer_c(xp, xs, cache_kv, cache_kr, layer, norm_g, w_in, qa_g, w_qb, kva_g, w_kvb, q_g, k_g,
             w_out):
    n, s, d = xp.shape
    o1 = C_Q_LORA
    o2 = o1 + C_KV_LORA
    o3 = o2 + C_ROPE
    kr_cols = _rope_padded(jnp.pad(w_in[:, o2:o3], ((0, 0), (C_NOPE, 0))))
    w = jnp.concatenate([w_in[:, :o2], w_in[:, o3:], kr_cols], axis=1).astype(BF16)
    wqb = _rope_padded(w_qb.reshape(C_Q_LORA, C_HEADS, C_QK))
    wqb = wqb.reshape(C_Q_LORA, C_HEADS * LANES).astype(BF16)
    kvb = w_kvb.reshape(C_KV_LORA, C_HEADS, C_NOPE + C_V)
    wk = jnp.pad(kvb[..., :C_NOPE], ((0, 0), (0, 0), (0, LANES - C_NOPE)))
    wv = kvb[..., C_NOPE:].reshape(C_KV_LORA, C_HEADS // 2, 2, C_V)
    zv = jnp.zeros_like(wv[:, :, 0])
    wv = jnp.stack([jnp.concatenate([wv[:, :, 0], zv], axis=-1),
                    jnp.concatenate([zv, wv[:, :, 1]], axis=-1)], axis=2)
    wkv = jnp.concatenate([wk.reshape(C_KV_LORA, -1), wv.reshape(C_KV_LORA, -1)], axis=1).astype(BF16)
    wo = w_out.astype(BF16)

    gn, qag, kvag = norm_g.reshape(1, -1), qa_g.reshape(1, -1), kva_g.reshape(1, -1)
    qg = _rope_padded(q_g.reshape(1, -1))
    krg = _rope_padded(jnp.pad(k_g[C_NOPE:].reshape(1, -1), ((0, 0), (C_NOPE, 0))))
    kg = jnp.pad(k_g[:C_NOPE].reshape(1, -1), ((0, 0), (0, LANES - C_NOPE)))

    q, ckv_p, kr_p, g = _proj_c(xp, jnp.arange(s), gn, w, qag, wqb, kvag, qg, krg)
    yp = _c_attn(q, ckv_p, kr_p, wkv, kg, g, xp, wo)

    n_layers, ns, past = cache_kv.shape[:3]
    ts = xs.shape[1]
    xs2 = xs.reshape(1, ns * ts, d)
    pos = past + jnp.arange(ts)
    q, ckv_s, kr_s, g = _proj_c(xs2, jnp.tile(pos, ns), gn, w, qag, wqb, kvag, qg, krg)
    h = _c_attn_sample(q, cache_kv.reshape(n_layers * ns, past, C_KV_LORA),
                       cache_kr.reshape(n_layers * ns, past, C_ROPE), layer * ns,
                       ckv_s, kr_s, wkv, kg, g, ns)
    ys = _out_proj(xs2, h, wo).reshape(xs.shape)
    rope = slice(C_NOPE, C_QK)
    return (yp, ys, ckv_p, kr_p[..., rope], ckv_s.reshape(ns, ts, C_KV_LORA),
            kr_s.reshape(ns, ts, LANES)[..., rope])


def kernel(x_prompt, x_sample, cache_a_k, cache_a_v, cache_b_k, cache_b_v, cache_c_kv, cache_c_kr,
           a_norm, a_w_in, a_q_norm, a_k_norm, a_rel_bias, a_w_out,
           b_norm, b_w_in, b_q_norm, b_k_norm, b_sinks, b_w_out,
           c_norm, c_w_in, c_q_a_norm, c_w_qb, c_kv_a_norm, c_w_kvb, c_q_norm, c_k_norm, c_w_out):
    depth = a_norm.shape[0] + b_norm.shape[0] + c_norm.shape[0]
    xp, xs = x_prompt, x_sample
    outs = [[] for _ in range(12)]
    a_k_all = a_v_all = None
    for layer in range(depth):
        j, kind = divmod(layer, N_MIXERS)
        if kind == 0:
            res = _layer_a(xp, xs, cache_a_k, cache_a_v, j, a_k_all, a_v_all, a_norm[j], a_w_in[j],
                           a_q_norm[j], a_k_norm[j], a_rel_bias[j], a_w_out[j])
            a_k_all, a_v_all = res[4], res[5]
        elif kind == 1:
            res = _layer_b(xp, xs, cache_b_k[j], cache_b_v[j], b_norm[j], b_w_in[j], b_q_norm[j],
                           b_k_norm[j], b_sinks[j], b_w_out[j])
        else:
            res = _layer_c(xp, xs, cache_c_kv, cache_c_kr, j, c_norm[j], c_w_in[j], c_q_a_norm[j],
                           c_w_qb[j], c_kv_a_norm[j], c_w_kvb[j], c_q_norm[j], c_k_norm[j],
                           c_w_out[j])
        xp, xs = res[0], res[1]
        for slot in range(4):
            outs[4 * kind + slot].append(res[2 + slot])
    outs[2], outs[3] = None, None
    stacked = [None if o is None else jnp.stack(o) for o in outs]
    stacked[2], stacked[3] = a_k_all, a_v_all
    order = [0, 1, 4, 5, 8, 9, 2, 3, 6, 7, 10, 11]
    return (xp, xs) + tuple(stacked[i] for i in order)
```

```python
import functools
import math

import jax
import jax.numpy as jnp
from jax import lax
from jax.experimental import pallas as pl
from jax.experimental.pallas import tpu as pltpu

F32 = jnp.float32
BF16 = jnp.bfloat16

LANES = 128
CHUNK = 64
NORM_EPS = 1e-6
NEG_INF = -1e30
LOG2E = math.log2(math.e)
N_MIXERS = 3

A_HEADS = 8
A_HEAD_DIM = 128
A_BAND_CHUNKS = 9
A_REL_CLIP = 128
A_REACH = (A_BAND_CHUNKS - 1) * CHUNK

B_HEADS = 16
B_KV_HEADS = 4
B_HEAD_DIM = 64
B_BAND_CHUNKS = 3
B_REACH = (B_BAND_CHUNKS - 1) * CHUNK

C_HEADS = 16
C_NOPE = 64
C_ROPE = 32
C_QK = C_NOPE + C_ROPE
C_V = 64
C_Q_LORA = 512
C_KV_LORA = 256
ROPE_THETA = 10000.0

ROW_TILE = 512
OUT_ROW_TILE = 1024
ATTN_Q_BLOCK = 256
VMEM_LIMIT = 56 * 1024 * 1024
C_ATTN_VMEM_LIMIT = 60 * 1024 * 1024


def _params(n_axes, vmem_limit=VMEM_LIMIT):
    return pltpu.CompilerParams(dimension_semantics=("arbitrary",) * n_axes,
                                vmem_limit_bytes=vmem_limit)


def _const_spec(shape):
    zeros = (0,) * len(shape)
    return pl.BlockSpec(shape, lambda *_: zeros)


def _rms(x, width):
    return x * lax.rsqrt(jnp.sum(x * x, axis=-1, keepdims=True) * (1.0 / width) + NORM_EPS)


def _normed_input(x_ref, gn_ref):
    x = x_ref[0]
    return (_rms(x, x.shape[-1]) * gn_ref[...]).astype(BF16)


def _tile(val, t):
    return val[:, t * LANES:(t + 1) * LANES]


def _silu(g):
    return g / (1.0 + jnp.exp(-g))


def _qk(q, k):
    return lax.dot_general(q, k, (((1,), (1,)), ((), ())), preferred_element_type=F32)


def _proj_a_kernel(x_ref, gn_ref, w_ref, qg_ref, kg_ref,
                   q_ref, k_ref, v_ref, g_ref, kf_ref, vf_ref, *, scale, tail):
    xn = _normed_input(x_ref, gn_ref)
    width = A_HEADS * A_HEAD_DIM
    rows = xn.shape[0]
    zq = jnp.dot(xn, w_ref[:, 0:width], preferred_element_type=F32)
    for h in range(A_HEADS):
        qn = _rms(_tile(zq, h), A_HEAD_DIM) * qg_ref[...]
        q_ref[0, :, h * LANES:(h + 1) * LANES] = (qn * scale).astype(BF16)
    zk = jnp.dot(xn, w_ref[:, width:2 * width], preferred_element_type=F32)
    for h in range(A_HEADS):
        kn = _rms(_tile(zk, h), A_HEAD_DIM) * kg_ref[...]
        k_ref[0, :, h * LANES:(h + 1) * LANES] = kn.astype(BF16)
        kf_ref[0, :, h * LANES:(h + 1) * LANES] = kn[rows - tail:, :]
    zv = jnp.dot(xn, w_ref[:, 2 * width:3 * width], preferred_element_type=F32)
    v_ref[0] = zv.astype(BF16)
    vf_ref[0] = zv[rows - tail:, :]
    g_ref[0] = jnp.dot(xn, w_ref[:, 3 * width:4 * width], preferred_element_type=F32)


def _half_rms(z, lo, width):
    sq = z * z
    s_lo = jnp.sum(jnp.where(lo, sq, 0.0), axis=-1, keepdims=True)
    s_hi = jnp.sum(jnp.where(lo, 0.0, sq), axis=-1, keepdims=True)
    r = jnp.where(lo, lax.rsqrt(s_lo * (1.0 / width) + NORM_EPS),
                  lax.rsqrt(s_hi * (1.0 / width) + NORM_EPS))
    return z * r


def _expand_halves(t, lo, out_ref, u):
    rolled = pltpu.roll(t, B_HEAD_DIM, 1)
    zero = jnp.zeros_like(t)
    tiles = (jnp.where(lo, t, zero), jnp.where(lo, zero, rolled),
             jnp.where(lo, rolled, zero), jnp.where(lo, zero, t))
    for i, val in enumerate(tiles):
        c = 4 * u + i
        out_ref[0, :, c * LANES:(c + 1) * LANES] = val.astype(BF16)


def _proj_b_kernel(x_ref, gn_ref, w_ref, qg_ref, kg_ref,
                   q_ref, k_ref, v_ref, g_ref, kf_ref, vf_ref, *, scale, tail):
    xn = _normed_input(x_ref, gn_ref)
    rows = xn.shape[0]
    qw = B_HEADS * B_HEAD_DIM
    kw = B_KV_HEADS * B_HEAD_DIM
    lo = lax.broadcasted_iota(jnp.int32, (rows, LANES), 1) < B_HEAD_DIM
    zq = jnp.dot(xn, w_ref[:, 0:qw], preferred_element_type=F32)
    for t in range(qw // LANES):
        qn = _half_rms(_tile(zq, t), lo, B_HEAD_DIM) * qg_ref[...]
        q_ref[0, :, t * LANES:(t + 1) * LANES] = (qn * scale).astype(BF16)
    zk = jnp.dot(xn, w_ref[:, qw:qw + kw], preferred_element_type=F32)
    for u in range(kw // LANES):
        kn = _half_rms(_tile(zk, u), lo, B_HEAD_DIM) * kg_ref[...]
        kf_ref[0, :, u * LANES:(u + 1) * LANES] = kn[rows - tail:, :]
        _expand_halves(kn, lo, k_ref, u)
    zv = jnp.dot(xn, w_ref[:, qw + kw:qw + 2 * kw], preferred_element_type=F32)
    vf_ref[0] = zv[rows - tail:, :]
    for u in range(kw // LANES):
        _expand_halves(_tile(zv, u), lo, v_ref, u)
    g_ref[0] = jnp.dot(xn, w_ref[:, qw + 2 * kw:2 * qw + 2 * kw], preferred_element_type=F32)


def _rotate(x, cos_ref, sin_ref):
    return x * cos_ref[...] + pltpu.roll(x, LANES - C_ROPE // 2, 1) * sin_ref[...]


def _proj_c_kernel(x_ref, gn_ref, w_ref, qag_ref, wqb_ref, kvag_ref, qg_ref, krg_ref,
                   cos_ref, sin_ref, q_ref, ckv_ref, kr_ref, g_ref, *, scale):
    xn = _normed_input(x_ref, gn_ref)
    rows = xn.shape[0]
    o1 = C_Q_LORA
    o2 = o1 + C_KV_LORA
    o3 = o2 + C_HEADS * C_V
    lane = lax.broadcasted_iota(jnp.int32, (rows, LANES), 1)
    nope = lane < C_NOPE
    rope = (lane >= C_NOPE) & (lane < C_QK)
    za = jnp.dot(xn, w_ref[:, 0:o1], preferred_element_type=F32)
    qa = (_rms(za, C_Q_LORA) * qag_ref[...]).astype(BF16)
    zq = jnp.dot(qa, wqb_ref[...], preferred_element_type=F32)
    for h in range(C_HEADS):
        z = _tile(zq, h)
        sq = z * z
        s_n = jnp.sum(jnp.where(nope, sq, 0.0), axis=-1, keepdims=True)
        s_r = jnp.sum(jnp.where(rope, sq, 0.0), axis=-1, keepdims=True)
        r = jnp.where(nope, lax.rsqrt(s_n * (1.0 / C_NOPE) + NORM_EPS),
                      lax.rsqrt(s_r * (1.0 / C_ROPE) + NORM_EPS))
        qn = _rotate(z * r * qg_ref[...], cos_ref, sin_ref)
        q_ref[0, :, h * LANES:(h + 1) * LANES] = (qn * scale).astype(BF16)
    zc = jnp.dot(xn, w_ref[:, o1:o2], preferred_element_type=F32)
    ckv_ref[0] = _rms(zc, C_KV_LORA) * kvag_ref[...]
    g_ref[0] = jnp.dot(xn, w_ref[:, o2:o3], preferred_element_type=F32)
    zr = jnp.dot(xn, w_ref[:, o3:o3 + LANES], preferred_element_type=F32)
    s_k = jnp.sum(jnp.where(rope, zr * zr, 0.0), axis=-1, keepdims=True)
    krn = zr * lax.rsqrt(s_k * (1.0 / C_ROPE) + NORM_EPS) * krg_ref[...]
    kr_ref[0] = _rotate(krn, cos_ref, sin_ref)


def _expand_rows(ckv, kr, w_ref, kg_ref, k_scr, v_scr, rows_at):
    ckv = ckv.astype(BF16)
    width = C_HEADS * LANES
    lane = lax.broadcasted_iota(jnp.int32, (ckv.shape[0], LANES), 1)
    zk = jnp.dot(ckv, w_ref[:, 0:width], preferred_element_type=F32)
    for h in range(C_HEADS):
        kn = _rms(_tile(zk, h), C_NOPE) * kg_ref[...] + kr
        k_scr[rows_at, h * LANES:(h + 1) * LANES] = kn.astype(BF16)
    zv = jnp.dot(ckv, w_ref[:, width:2 * width], preferred_element_type=F32)
    for h in range(C_HEADS):
        ones_lane = C_V if h % 2 == 0 else 0
        v_scr[rows_at, h * LANES:(h + 1) * LANES] = jnp.where(lane == ones_lane, 1.0,
                                                              _tile(zv, h)).astype(BF16)


def _out_proj_kernel(x_ref, h_ref, w_ref, y_ref):
    y_ref[0] = x_ref[0] + jnp.dot(h_ref[0], w_ref[...], preferred_element_type=F32)


def _row_reduce(pieces, combine, reduce):
    tiles, narrow = [], []
    for piece in pieces:
        width = piece.shape[1]
        if width % LANES == 0:
            tiles += [piece[:, i * LANES:(i + 1) * LANES] for i in range(width // LANES)]
        else:
            narrow.append(piece)
    folded = ([functools.reduce(combine, tiles)] if tiles else []) + narrow
    return functools.reduce(combine, [reduce(x, axis=-1, keepdims=True) for x in folded])


def _softmax_parts(scores, sink):
    m = _row_reduce(scores, jnp.maximum, jnp.max)
    if sink is not None:
        m = jnp.maximum(m, sink)
    probs = [jnp.exp2(s - m) for s in scores]
    denom = _row_reduce(probs, jnp.add, jnp.sum)
    if sink is not None:
        denom = denom + jnp.exp2(sink - m)
    return probs, denom


def _window_attn_kernel(*refs, n_pieces, heads, need, has_bias, has_sinks):
    q_ref = refs[0]
    k_refs = refs[1:1 + n_pieces]
    v_refs = refs[1 + n_pieces:1 + 2 * n_pieces]
    pos = 1 + 2 * n_pieces
    g_ref = refs[pos]
    pos += 1
    bias_ref = None
    if has_bias:
        bias_ref = refs[pos]
        pos += 1
    sink_ref = None
    if has_sinks:
        sink_ref = refs[pos]
        pos += 1
    h_ref = refs[pos]

    step = pl.program_id(1)
    rows = q_ref.shape[1]
    lo = lax.broadcasted_iota(jnp.int32, (rows, LANES), 1) < LANES // 2
    pen = [jnp.where(step >= nd, 0.0, NEG_INF).astype(F32) if nd > 0 else None for nd in need]
    for t in range(len(heads)):
        acc = None
        for sub, (q_t, kv_t) in enumerate(heads[t]):
            head = t * len(heads[t]) + sub
            qt = q_ref[0, :, q_t * LANES:(q_t + 1) * LANES]
            scores = []
            off = 0
            for j in range(n_pieces):
                kj = k_refs[j][0, :, kv_t * LANES:(kv_t + 1) * LANES]
                s = _qk(qt, kj)
                if has_bias:
                    s = s + bias_ref[head, :, off:off + kj.shape[0]]
                if pen[j] is not None:
                    s = s + pen[j]
                scores.append(s)
                off += kj.shape[0]
            probs, denom = _softmax_parts(scores, sink_ref[head] if has_sinks else None)
            o = None
            for j in range(n_pieces):
                vj = v_refs[j][0, :, kv_t * LANES:(kv_t + 1) * LANES]
                pv = jnp.dot(probs[j].astype(BF16), vj, preferred_element_type=F32)
                o = pv if o is None else o + pv
            o = o / denom
            acc = o if acc is None else jnp.where(lo, acc, o)
        g = g_ref[0, :, t * LANES:(t + 1) * LANES]
        h_ref[0, :, t * LANES:(t + 1) * LANES] = (acc * _silu(g)).astype(BF16)


def _a_sample_kernel(q_ref, kn_ref, vn_ref, knf_ref, vnf_ref, ck_ref, cv_ref, g_ref, bias_ref,
                     *rest, slot):
    h_ref, ok_ref, ov_ref = rest[-3:]
    past = ck_ref.shape[2]
    new = q_ref.shape[1]
    reach = ok_ref.shape[2]
    drop = past + new - reach
    for other in range(ok_ref.shape[0]):
        if other != slot:
            ok_ref[other] = jnp.zeros(ok_ref.shape[1:], F32)
            ov_ref[other] = jnp.zeros(ov_ref.shape[1:], F32)
    k_heads = jnp.swapaxes(ck_ref[0, 0], 0, 1)
    v_heads = jnp.swapaxes(cv_ref[0, 0], 0, 1)
    k_out, v_out = [], []
    for h in range(A_HEADS):
        cols = slice(h * LANES, (h + 1) * LANES)
        kc = k_heads[h]
        vc = v_heads[h]
        qh = q_ref[0, :, cols]
        s_c = _qk(qh, kc.astype(BF16)) + bias_ref[h, :, 0:past]
        s_n = _qk(qh, kn_ref[0, :, cols]) + bias_ref[h, :, past:past + new]
        (p_c, p_n), denom = _softmax_parts([s_c, s_n], None)
        o = (jnp.dot(p_c.astype(BF16), vc.astype(BF16), preferred_element_type=F32)
             + jnp.dot(p_n.astype(BF16), vn_ref[0, :, cols], preferred_element_type=F32)) / denom
        h_ref[0, :, cols] = (o * _silu(g_ref[0, :, cols])).astype(BF16)
        k_out.append(jnp.concatenate([kc[drop:, :], knf_ref[0, :, cols]], axis=0))
        v_out.append(jnp.concatenate([vc[drop:, :], vnf_ref[0, :, cols]], axis=0))
    ok_ref[slot, 0] = jnp.swapaxes(jnp.stack(k_out), 0, 1)
    ov_ref[slot, 0] = jnp.swapaxes(jnp.stack(v_out), 0, 1)


def _online_update(c, q, kj, vj, bias, m_scr, acc_scr):
    s = _qk(q, kj)
    if bias is not None:
        s = s + bias
    parts = [s[:, i * LANES:(i + 1) * LANES] for i in range(s.shape[1] // LANES)]
    m_old = m_scr[c]
    m_new = jnp.maximum(m_old, jnp.max(functools.reduce(jnp.maximum, parts), axis=-1,
                                       keepdims=True))
    m_scr[c] = m_new
    p = jnp.concatenate([jnp.exp2(x - m_new).astype(BF16) for x in parts], axis=1)
    acc_scr[c] = (acc_scr[c] * jnp.exp2(m_old - m_new)
                  + jnp.dot(p, vj, preferred_element_type=F32))


def _band_attn_kernel(*refs, n_pieces, heads, need, has_sinks):
    q_ref = refs[0]
    k_refs = refs[1:1 + n_pieces]
    v_refs = refs[1 + n_pieces:1 + 2 * n_pieces]
    g_ref, bias_ref = refs[1 + 2 * n_pieces:3 + 2 * n_pieces]
    pos = 3 + 2 * n_pieces
    sink_ref = None
    if has_sinks:
        sink_ref = refs[pos]
        pos += 1
    x_ref, wo_ref, y_ref, s_scr, m_scr, h_scr = refs[pos:pos + 6]

    step = pl.program_id(1)
    rows = q_ref.shape[1]
    lo = lax.broadcasted_iota(jnp.int32, (rows, LANES), 1) < LANES // 2
    flat = [(t, sub, q_t, kv_t) for t in range(len(heads)) for sub, (q_t, kv_t) in enumerate(heads[t])]
    per_tile = len(heads[0])

    offsets = [sum(k_refs[i].shape[1] for i in range(j)) for j in range(n_pieces)]

    for j in range(n_pieces):
        width = k_refs[j].shape[1]
        pen = jnp.where(step >= need[j], 0.0, NEG_INF).astype(F32) if need[j] > 0 else None
        for t, sub, q_t, kv_t in flat:
            c = t * per_tile + sub
            s = _qk(q_ref[0, :, q_t * LANES:(q_t + 1) * LANES],
                    k_refs[j][0, :, kv_t * LANES:(kv_t + 1) * LANES])
            s = s + bias_ref[c, :, offsets[j]:offsets[j] + width]
            if pen is not None:
                s = s + pen
            s_scr[c, :, offsets[j]:offsets[j] + width] = s
            tiles = [s[:, i * LANES:(i + 1) * LANES] for i in range(width // LANES)]
            if j > 0:
                tiles.append(m_scr[c])
            m_scr[c] = functools.reduce(jnp.maximum, tiles)

    slot0 = jnp.minimum(step, 0)
    for t in range(len(heads)):
        out = None
        for sub, (_, kv_t) in enumerate(heads[t]):
            c = t * per_tile + sub
            m = jnp.max(m_scr[c + slot0], axis=-1, keepdims=True)
            if has_sinks:
                m = jnp.maximum(m, sink_ref[c])
            lanes_sum = None
            o = None
            for j in range(n_pieces):
                width = k_refs[j].shape[1]
                p = jnp.exp2(s_scr[c + slot0, :, offsets[j]:offsets[j] + width] - m)
                for i in range(width // LANES):
                    tile = p[:, i * LANES:(i + 1) * LANES]
                    lanes_sum = tile if lanes_sum is None else lanes_sum + tile
                pv = jnp.dot(p.astype(BF16), v_refs[j][0, :, kv_t * LANES:(kv_t + 1) * LANES],
                             preferred_element_type=F32)
                o = pv if o is None else o + pv
            denom = jnp.sum(lanes_sum, axis=-1, keepdims=True)
            if has_sinks:
                denom = denom + jnp.exp2(sink_ref[c] - m)
            o = o / denom
            out = o if out is None else jnp.where(lo, out, o)
        g = g_ref[0, :, t * LANES:(t + 1) * LANES]
        h_scr[:, t * LANES:(t + 1) * LANES] = (out * _silu(g)).astype(BF16)

    y_ref[0] = x_ref[0] + jnp.dot(h_scr[...], wo_ref[...], preferred_element_type=F32)


def _c_attn_kernel(q_ref, ckv_ref, kr_ref, w_ref, kg_ref, g_ref, mask_ref, x_ref, wo_ref, y_ref,
                   k_scr, v_scr, m_scr, acc_scr, h_scr, *, kb, expand_tile):
    step = pl.program_id(1)
    rows = q_ref.shape[1]
    lane = lax.broadcasted_iota(jnp.int32, (rows, LANES), 1)
    lo = lane < C_V
    per_q = rows // kb

    @pl.when(step == 0)
    def _expand_sequence():
        def expand(r, carry):
            at = pl.ds(pl.multiple_of(r * expand_tile, expand_tile), expand_tile)
            _expand_rows(ckv_ref[0, at, :], kr_ref[0, at, :], w_ref, kg_ref, k_scr, v_scr, at)
            return carry
        lax.fori_loop(0, ckv_ref.shape[1] // expand_tile, expand, 0)

    m_scr[...] = jnp.full(m_scr.shape, NEG_INF, F32)
    acc_scr[...] = jnp.zeros(acc_scr.shape, F32)

    def block(j, masked):
        start = pl.multiple_of(j * kb, kb)
        for c in range(C_HEADS):
            cols = slice(c * LANES, (c + 1) * LANES)
            _online_update(c, q_ref[0, :, cols], k_scr[pl.ds(start, kb), cols],
                           v_scr[pl.ds(start, kb), cols],
                           None if masked is None else mask_ref[masked], m_scr, acc_scr)

    def body(j, carry):
        block(j, None)
        return carry

    lax.fori_loop(0, step * per_q, body, 0)
    for d in range(per_q):
        block(step * per_q + d, d)

    for t in range(C_HEADS // 2):
        a_e = acc_scr[2 * t]
        a_o = acc_scr[2 * t + 1]
        l_e = jnp.sum(jnp.where(lane == C_V, a_e, 0.0), axis=-1, keepdims=True)
        l_o = jnp.sum(jnp.where(lane == 0, a_o, 0.0), axis=-1, keepdims=True)
        g = g_ref[0, :, t * LANES:(t + 1) * LANES]
        o = jnp.where(lo, a_e / l_e, a_o / l_o)
        h_scr[:, t * LANES:(t + 1) * LANES] = (o * _silu(g)).astype(BF16)

    y_ref[0] = x_ref[0] + jnp.dot(h_scr[...], wo_ref[...], preferred_element_type=F32)


def _c_sample_kernel(q_ref, ckv_ref, ckr_ref, nkv_ref, nkr_ref, w_ref, kg_ref, place_ref, g_ref,
                     h_ref, k_scr, v_scr, *, expand_tile):
    past = ckv_ref.shape[1]
    new = q_ref.shape[1]
    lo = lax.broadcasted_iota(jnp.int32, (new, LANES), 1) < C_V

    def expand(r, carry):
        at = pl.ds(pl.multiple_of(r * expand_tile, expand_tile), expand_tile)
        kr = jnp.dot(ckr_ref[0, at, :].astype(BF16), place_ref[...], preferred_element_type=F32)
        _expand_rows(ckv_ref[0, at, :], kr, w_ref, kg_ref, k_scr, v_scr, at)
        return carry
    lax.fori_loop(0, past // expand_tile, expand, 0)
    _expand_rows(nkv_ref[0], nkr_ref[0], w_ref, kg_ref, k_scr, v_scr, pl.ds(past, new))

    for t in range(C_HEADS // 2):
        acc = None
        for c in (2 * t, 2 * t + 1):
            cols = slice(c * LANES, (c + 1) * LANES)
            q = q_ref[0, :, cols]
            s_c = _qk(q, k_scr[0:past, cols])
            s_n = _qk(q, k_scr[past:past + new, cols])
            (p_c, p_n), denom = _softmax_parts([s_c, s_n], None)
            o = (jnp.dot(p_c.astype(BF16), v_scr[0:past, cols], preferred_element_type=F32)
                 + jnp.dot(p_n.astype(BF16), v_scr[past:past + new, cols],
                           preferred_element_type=F32)) / denom
            acc = o if acc is None else jnp.where(lo, acc, o)
        g = g_ref[0, :, t * LANES:(t + 1) * LANES]
        h_ref[0, :, t * LANES:(t + 1) * LANES] = (acc * _silu(g)).astype(BF16)


def _rows(tm, width):
    return pl.BlockSpec((1, tm, width), lambda b, i: (b, i, 0))


def _cache_out(n, s, tm, width, tail):
    if tail is None:
        return jax.ShapeDtypeStruct((n, s, width), F32), _rows(tm, width), tm
    assert tail <= tm and s % tm == 0
    spec = pl.BlockSpec((1, tail, width), lambda b, i: (b, 0, 0))
    return jax.ShapeDtypeStruct((n, tail, width), F32), spec, tail


def _row_tile(s):
    tm = min(ROW_TILE, s)
    assert s % tm == 0
    return tm


def _proj_ab(kernel, x, gn, w, qg, kg, *, q_width, kv_width, g_width, scale, tail, name):
    n, s, d = x.shape
    tm = _row_tile(s)
    cache_width = (w.shape[1] - q_width - g_width) // 2
    cache_shape, cache_spec, tail_rows = _cache_out(n, s, tm, cache_width, tail)
    return pl.pallas_call(
        functools.partial(kernel, scale=scale, tail=tail_rows),
        grid=(n, s // tm),
        in_specs=[_rows(tm, d), _const_spec(gn.shape), _const_spec(w.shape),
                  _const_spec(qg.shape), _const_spec(kg.shape)],
        out_specs=[_rows(tm, q_width), _rows(tm, kv_width), _rows(tm, kv_width), _rows(tm, g_width),
                   cache_spec, cache_spec],
        out_shape=[jax.ShapeDtypeStruct((n, s, q_width), BF16),
                   jax.ShapeDtypeStruct((n, s, kv_width), BF16),
                   jax.ShapeDtypeStruct((n, s, kv_width), BF16),
                   jax.ShapeDtypeStruct((n, s, g_width), F32), cache_shape, cache_shape],
        compiler_params=_params(2), name=name)(x, gn, w, qg, kg)


def _out_proj(x, h, w):
    n, s, d = x.shape
    tm = math.gcd(s, OUT_ROW_TILE)
    return pl.pallas_call(
        _out_proj_kernel, grid=(n, s // tm),
        in_specs=[_rows(tm, d), _rows(tm, h.shape[-1]), _const_spec(w.shape)],
        out_specs=_rows(tm, d), out_shape=jax.ShapeDtypeStruct((n, s, d), F32),
        compiler_params=_params(2), name="out_proj")(x, h, w)


def _window_attn(q, k_pieces, v_pieces, g, bias, sinks, *, grid, q_map, q_block, heads, need, name):
    width = len(heads) * LANES
    in_specs = [pl.BlockSpec((1, q_block, q.shape[-1]), q_map)]
    args = [q]
    for arr, rows, imap in list(k_pieces) + list(v_pieces):
        in_specs.append(pl.BlockSpec((1, rows, arr.shape[-1]), imap))
        args.append(arr)
    in_specs.append(pl.BlockSpec((1, q_block, width), q_map))
    args.append(g)
    if bias is not None:
        in_specs.append(_const_spec(bias.shape))
        args.append(bias)
    if sinks is not None:
        in_specs.append(pl.BlockSpec(memory_space=pltpu.SMEM))
        args.append(sinks)
    kernel = functools.partial(_window_attn_kernel, n_pieces=len(k_pieces), heads=heads, need=need,
                               has_bias=bias is not None, has_sinks=sinks is not None)
    return pl.pallas_call(
        kernel, grid=grid, in_specs=in_specs,
        out_specs=pl.BlockSpec((1, q_block, width), q_map),
        out_shape=jax.ShapeDtypeStruct(g.shape[:2] + (width,), BF16),
        compiler_params=_params(2), name=name)(*args)


def _band_bias(base, q_rows, k_rows, q_off, band_chunks):
    m = q_rows + k_rows - 1
    vec = base(q_off + (q_rows - 1) - jnp.arange(m)) * LOG2E
    flat = jnp.tile(vec, (1, q_rows))[:, q_rows - 1:q_rows - 1 + q_rows * (m - 1)]
    bias = flat.reshape(-1, q_rows, m - 1)[:, :, :k_rows]
    if band_chunks is None:
        return bias
    i = jnp.arange(q_rows)[:, None]
    j = jnp.arange(k_rows)[None, :]
    lag = (q_off + i) // CHUNK - j // CHUNK
    return jnp.where((lag >= 0) & (lag < band_chunks), bias, NEG_INF)


def _band_prompt_pieces(arr, q_block, reach):
    pieces, need = [], []
    if reach % q_block == 0:
        for r in range(reach // q_block, 0, -1):
            pieces.append((arr, q_block, lambda b, i, r=r: (b, jnp.maximum(i - r, 0), 0)))
            need.append(r)
    else:
        assert q_block % reach == 0
        ratio = q_block // reach
        pieces.append((arr, reach, lambda b, i: (b, jnp.maximum(i * ratio - 1, 0), 0)))
        need.append(1)
    pieces.append((arr, q_block, lambda b, i: (b, i, 0)))
    need.append(0)
    return pieces, tuple(need)


def _band_prompt(proj, xp, wo, *, reach, band_chunks, heads, bias_base, sinks, name):
    n, s, d = xp.shape
    qb = min(ATTN_Q_BLOCK, s)
    assert s % qb == 0 and s >= reach
    q, k, v, g, kf, vf = proj(xp, reach)
    kp, need = _band_prompt_pieces(k, qb, reach)
    vp, _ = _band_prompt_pieces(v, qb, reach)
    bias = _band_bias(bias_base, qb, reach + qb, reach, band_chunks)
    width = len(heads) * LANES
    n_slots = sum(len(hs) for hs in heads)
    pieces = kp + vp
    args = [q] + [arr for arr, _, _ in pieces] + [g, bias]
    in_specs = ([_rows(qb, q.shape[-1])]
                + [pl.BlockSpec((1, rows, arr.shape[-1]), imap) for arr, rows, imap in pieces]
                + [_rows(qb, width), _single(bias.shape, lambda b, i: (0, 0, 0))])
    if sinks is not None:
        in_specs.append(pl.BlockSpec(memory_space=pltpu.SMEM))
        args.append(sinks)
    in_specs += [_rows(qb, d), _single(wo.shape, lambda b, i: (0, 0))]
    args += [xp, wo]
    scratch = [pltpu.VMEM((n_slots, qb, reach + qb), F32), pltpu.VMEM((n_slots, qb, LANES), F32),
               pltpu.VMEM((qb, width), BF16)]
    yp = pl.pallas_call(
        functools.partial(_band_attn_kernel, n_pieces=len(kp), heads=heads, need=need,
                          has_sinks=sinks is not None),
        grid=(n, s // qb), in_specs=in_specs, out_specs=_rows(qb, d),
        out_shape=jax.ShapeDtypeStruct((n, s, d), F32),
        scratch_shapes=scratch,
        compiler_params=_params(2), name=name + "_attn_prompt")(*args)
    return yp, kf, vf


def _layer_a(xp, xs, cache_k, cache_v, layer, prev_k, prev_v, norm_g, w_in, q_g, k_g, rel, w_out):
    width = A_HEADS * A_HEAD_DIM
    w = w_in.astype(BF16)
    wo = w_out.astype(BF16)
    gn, qg, kg = norm_g.reshape(1, -1), q_g.reshape(1, -1), k_g.reshape(1, -1)

    head_shape = (A_HEADS, A_HEAD_DIM)

    def proj(x, tail):
        return _proj_ab(_proj_a_kernel, x, gn, w, qg, kg, q_width=width, kv_width=width,
                        g_width=width, scale=A_HEAD_DIM ** -0.5 * LOG2E, tail=tail, name="proj_a")

    def bias_base(dist):
        idx = jnp.clip(dist, -A_REL_CLIP, A_REL_CLIP) + A_REL_CLIP
        return rel.astype(F32)[:, idx]

    heads = tuple(((t, t),) for t in range(A_HEADS))
    yp, kf, vf = _band_prompt(proj, xp, wo, reach=A_REACH, band_chunks=A_BAND_CHUNKS, heads=heads,
                              bias_base=bias_base, sinks=None, name="a")
    n = xp.shape[0]
    kf = kf.reshape((n, A_REACH) + head_shape)
    vf = vf.reshape((n, A_REACH) + head_shape)

    n_layers, ns, past = cache_k.shape[:3]
    ts, d = xs.shape[1:]
    xs2 = xs.reshape(1, ns * ts, d)
    q, k, v, g, knf, vnf = proj(xs2, None)
    bias_s = _band_bias(bias_base, ts, past + ts, past, None)
    new = lambda w_: pl.BlockSpec((1, ts, w_), lambda b, i: (0, i, 0))
    cache_block = lambda rows: pl.BlockSpec((1, 1, rows) + head_shape,
                                            lambda b, i: (layer, i, 0, 0, 0))
    out_cache = jax.ShapeDtypeStruct((n_layers, ns, A_REACH) + head_shape, F32)
    prev = [] if prev_k is None else [prev_k, prev_v]
    if prev:
        out_block, slot = cache_block(A_REACH), 0
    else:
        out_block = pl.BlockSpec((n_layers, 1, A_REACH) + head_shape, lambda b, i: (0, i, 0, 0, 0))
        slot = layer
    n_in = 9
    h, k_all, v_all = pl.pallas_call(
        functools.partial(_a_sample_kernel, slot=slot), grid=(1, ns),
        in_specs=[new(width)] * 5 + [cache_block(past)] * 2 + [new(width), _const_spec(bias_s.shape)]
        + [pl.BlockSpec(memory_space=pl.ANY)] * len(prev),
        out_specs=[new(width), out_block, out_block],
        out_shape=[jax.ShapeDtypeStruct((1, ns * ts, width), BF16), out_cache, out_cache],
        input_output_aliases={n_in + i: 1 + i for i in range(len(prev))},
        compiler_params=_params(2), name="a_attn_sample")(
            q, k, v, knf, vnf, cache_k, cache_v, g, bias_s, *prev)
    ys = _out_proj(xs2, h, wo).reshape(xs.shape)
    return yp, ys, kf, vf, k_all, v_all


def _layer_b(xp, xs, ck, cv, norm_g, w_in, q_g, k_g, sinks, w_out):
    qw = B_HEADS * B_HEAD_DIM
    kw = B_KV_HEADS * B_HEAD_DIM
    w = w_in.astype(BF16)
    wo = w_out.astype(BF16)
    gn = norm_g.reshape(1, -1)
    qg = jnp.tile(q_g.reshape(1, -1), (1, 2))
    kg = jnp.tile(k_g.reshape(1, -1), (1, 2))
    sinks2 = sinks.astype(F32) * LOG2E

    def proj(x, tail):
        return _proj_ab(_proj_b_kernel, x, gn, w, qg, kg, q_width=qw, kv_width=4 * kw, g_width=qw,
                        scale=B_HEAD_DIM ** -0.5 * LOG2E, tail=tail, name="proj_b")

    slopes = 2.0 ** (-8.0 * jnp.arange(1, B_HEADS + 1, dtype=F32) / B_HEADS)

    def bias_base(dist):
        return -slopes[:, None] * jnp.abs(dist).astype(F32)[None, :]

    def expand_cache(c):
        n, rows, _ = c.shape
        c4 = c.reshape(n, rows, B_KV_HEADS, 1, B_HEAD_DIM)
        z = jnp.zeros_like(c4)
        lo = jnp.concatenate([c4, z], axis=-1)
        hi = jnp.concatenate([z, c4], axis=-1)
        return jnp.concatenate([lo, hi], axis=3).reshape(n, rows, 4 * kw).astype(BF16)

    heads = tuple(((t, 2 * (t // 2)), (t, 2 * (t // 2) + 1)) for t in range(qw // LANES))
    yp, kf, vf = _band_prompt(proj, xp, wo, reach=B_REACH, band_chunks=B_BAND_CHUNKS, heads=heads,
                              bias_base=bias_base, sinks=sinks2, name="b")
    n = xp.shape[0]
    head_shape = ck.shape[2:]

    ns, ts, d = xs.shape
    past = ck.shape[1]
    xs2 = xs.reshape(1, ns * ts, d)
    q, k, v, g, kf_s, vf_s = proj(xs2, None)
    ckf = ck.reshape(ns, past, -1)
    cvf = cv.reshape(ns, past, -1)
    cache_map = lambda b, i: (i, 0, 0)
    new_map = lambda b, i: (0, i, 0)
    bias_s = _band_bias(bias_base, ts, past + ts, past, None)
    h = _window_attn(q, [(expand_cache(ckf), past, cache_map), (k, ts, new_map)],
                     [(expand_cache(cvf), past, cache_map), (v, ts, new_map)], g, bias_s, sinks2,
                     grid=(1, ns), q_map=new_map, q_block=ts, heads=heads, need=(0, 0),
                     name="b_attn_sample")
    ys = _out_proj(xs2, h, wo).reshape(xs.shape)
    k_all = jnp.concatenate([ckf, kf_s.reshape(ns, ts, -1)], axis=1)[:, -B_REACH:]
    v_all = jnp.concatenate([cvf, vf_s.reshape(ns, ts, -1)], axis=1)[:, -B_REACH:]
    return (yp, ys, kf.reshape((n, B_REACH) + head_shape), vf.reshape((n, B_REACH) + head_shape),
            k_all.reshape((ns, B_REACH) + head_shape), v_all.reshape((ns, B_REACH) + head_shape))


def _rope_tables(pos):
    half = C_ROPE // 2
    inv = ROPE_THETA ** (-jnp.arange(half, dtype=F32) / half)
    ang = pos.astype(F32)[:, None] * inv[None, :]
    cos, sin = jnp.cos(ang), jnp.sin(ang)
    rows = pos.shape[0]
    ones = jnp.ones((rows, C_NOPE), F32)
    zeros = lambda width: jnp.zeros((rows, width), F32)
    pad = LANES - C_QK
    cos_t = jnp.concatenate([ones, cos, cos, zeros(pad)], axis=1)
    sin_t = jnp.concatenate([zeros(C_NOPE), -sin, sin, zeros(pad)], axis=1)
    return cos_t, sin_t


def _rope_padded(x):
    half = C_ROPE // 2
    zeros = jnp.zeros(x.shape[:-1] + (LANES - C_QK - half,), x.dtype)
    return jnp.concatenate([x, x[..., C_NOPE:C_NOPE + half], zeros], axis=-1)


def _proj_c(x, pos, gn, w, qag, wqb, kvag, qg, krg):
    n, s, d = x.shape
    tm = _row_tile(s)
    per_seq = pos.shape[0] // tm
    tables = _rope_tables(pos)
    table_spec = pl.BlockSpec((tm, LANES), lambda b, i: (i % per_seq, 0))
    consts = (gn, w, qag, wqb, kvag, qg, krg)
    return pl.pallas_call(
        functools.partial(_proj_c_kernel, scale=C_QK ** -0.5 * LOG2E),
        grid=(n, s // tm),
        in_specs=[_rows(tm, d)] + [_const_spec(c.shape) for c in consts] + [table_spec] * 2,
        out_specs=[_rows(tm, C_HEADS * LANES), _rows(tm, C_KV_LORA), _rows(tm, LANES),
                   _rows(tm, C_HEADS * C_V)],
        out_shape=[jax.ShapeDtypeStruct((n, s, C_HEADS * LANES), BF16),
                   jax.ShapeDtypeStruct((n, s, C_KV_LORA), F32),
                   jax.ShapeDtypeStruct((n, s, LANES), F32),
                   jax.ShapeDtypeStruct((n, s, C_HEADS * C_V), F32)],
        compiler_params=_params(2), name="proj_c")(x, *consts, *tables)


def _single(shape, index_map):
    return pl.BlockSpec(shape, index_map, pipeline_mode=pl.Buffered(1))


def _c_attn(q, ckv, kr, w, kg, g, x, wo):
    n, s, _ = q.shape
    qb = min(ATTN_Q_BLOCK, s)
    kb = qb
    assert s % qb == 0 and qb % kb == 0 and kb % CHUNK == 0
    per_q = qb // kb
    row_chunk = jnp.arange(qb)[None, :, None] // CHUNK
    col_chunk = (jnp.arange(per_q)[:, None, None] * kb + jnp.arange(kb)[None, None, :]) // CHUNK
    mask = jnp.where(col_chunk <= row_chunk, 0.0, NEG_INF).astype(F32)
    width = C_HEADS * LANES
    whole = lambda w_: _single((1, s, w_), lambda b, i: (b, 0, 0))
    const = lambda a: _single(a.shape, lambda b, i: (0,) * a.ndim)
    d = x.shape[-1]
    return pl.pallas_call(
        functools.partial(_c_attn_kernel, kb=kb, expand_tile=qb), grid=(n, s // qb),
        in_specs=[_rows(qb, width), whole(C_KV_LORA), whole(LANES), const(w), const(kg),
                  _rows(qb, g.shape[-1]), const(mask), _rows(qb, d), const(wo)],
        out_specs=_rows(qb, d),
        out_shape=jax.ShapeDtypeStruct(x.shape, F32),
        scratch_shapes=[pltpu.VMEM((s, width), BF16), pltpu.VMEM((s, width), BF16),
                        pltpu.VMEM((C_HEADS, qb, LANES), F32), pltpu.VMEM((C_HEADS, qb, LANES), F32),
                        pltpu.VMEM((qb, g.shape[-1]), BF16)],
        compiler_params=_params(2, C_ATTN_VMEM_LIMIT), name="c_attn_prompt")(
            q, ckv, kr, w, kg, g, mask, x, wo)


def _c_attn_sample(q, cache_kv, cache_kr, first, new_kv, new_kr, w, kg, g, ns):
    past = cache_kv.shape[1]
    ts = q.shape[1] // ns
    width = C_HEADS * LANES
    tile = math.gcd(past, ATTN_Q_BLOCK)
    place = jnp.pad(jnp.eye(C_ROPE, dtype=BF16), ((0, 0), (C_NOPE, LANES - C_QK)))
    new = lambda w_: pl.BlockSpec((1, ts, w_), lambda b, i: (0, i, 0))
    cached = lambda w_: pl.BlockSpec((1, past, w_), lambda b, i: (first + i, 0, 0))
    return pl.pallas_call(
        functools.partial(_c_sample_kernel, expand_tile=tile), grid=(1, ns),
        in_specs=[new(width), cached(C_KV_LORA), cached(C_ROPE), new(C_KV_LORA), new(LANES),
                  _const_spec(w.shape), _const_spec(kg.shape), _const_spec(place.shape),
                  new(g.shape[-1])],
        out_specs=new(g.shape[-1]),
        out_shape=jax.ShapeDtypeStruct(g.shape, BF16),
        scratch_shapes=[pltpu.VMEM((past + ts, width), BF16), pltpu.VMEM((past + ts, width), BF16)],
        compiler_params=_params(2), name="c_attn_sample")(
            q, cache_kv, cache_kr, new_kv, new_kr, w, kg, place, g)


def _layer_c(xp, xs, cache_kv, cache_kr, layer, norm_g, w_in, qa_g, w_qb, kva_g, w_kvb, q_g, k_g,
             w_out):
    n, s, d = xp.shape
    o1 = C_Q_LORA
    o2 = o1 + C_KV_LORA
    o3 = o2 + C_ROPE
    kr_cols = _rope_padded(jnp.pad(w_in[:, o2:o3], ((0, 0), (C_NOPE, 0))))
    w = jnp.concatenate([w_in[:, :o2], w_in[:, o3:], kr_cols], axis=1).astype(BF16)
    wqb = _rope_padded(w_qb.reshape(C_Q_LORA, C_HEADS, C_QK))
    wqb = wqb.reshape(C_Q_LORA, C_HEADS * LANES).astype(BF16)
    kvb = w_kvb.reshape(C_KV_LORA, C_HEADS, C_NOPE + C_V)
    wk = jnp.pad(kvb[..., :C_NOPE], ((0, 0), (0, 0), (0, LANES - C_NOPE)))
    wv = kvb[..., C_NOPE:].reshape(C_KV_LORA, C_HEADS // 2, 2, C_V)
    zv = jnp.zeros_like(wv[:, :, 0])
    wv = jnp.stack([jnp.concatenate([wv[:, :, 0], zv], axis=-1),
                    jnp.concatenate([zv, wv[:, :, 1]], axis=-1)], axis=2)
    wkv = jnp.concatenate([wk.reshape(C_KV_LORA, -1), wv.reshape(C_KV_LORA, -1)], axis=1).astype(BF16)
    wo = w_out.astype(BF16)

    gn, qag, kvag = norm_g.reshape(1, -1), qa_g.reshape(1, -1), kva_g.reshape(1, -1)
    qg = _rope_padded(q_g.reshape(1, -1))
    krg = _rope_padded(jnp.pad(k_g[C_NOPE:].reshape(1, -1), ((0, 0), (C_NOPE, 0))))
    kg = jnp.pad(k_g[:C_NOPE].reshape(1, -1), ((0, 0), (0, LANES - C_NOPE)))

    q, ckv_p, kr_p, g = _proj_c(xp, jnp.arange(s), gn, w, qag, wqb, kvag, qg, krg)
    yp = _c_attn(q, ckv_p, kr_p, wkv, kg, g, xp, wo)

    n_layers, ns, past = cache_kv.shape[:3]
    ts = xs.shape[1]
    xs2 = xs.reshape(1, ns * ts, d)
    pos = past + jnp.arange(ts)
    q, ckv_s, kr_s, g = _proj_c(xs2, jnp.tile(pos, ns), gn, w, qag, wqb, kvag, qg, krg)
    h = _c_attn_sample(q, cache_kv.reshape(n_layers * ns, past, C_KV_LORA),
                       cache_kr.reshape(n_layers * ns, past, C_ROPE), layer * ns,
                       ckv_s, kr_s, wkv, kg, g, ns)
    ys = _out_proj(xs2, h, wo).reshape(xs.shape)
    rope = slice(C_NOPE, C_QK)
    return (yp, ys, ckv_p, kr_p[..., rope], ckv_s.reshape(ns, ts, C_KV_LORA),
            kr_s.reshape(ns, ts, LANES)[..., rope])


def kernel(x_prompt, x_sample, cache_a_k, cache_a_v, cache_b_k, cache_b_v, cache_c_kv, cache_c_kr,
           a_norm, a_w_in, a_q_norm, a_k_norm, a_rel_bias, a_w_out,
           b_norm, b_w_in, b_q_norm, b_k_norm, b_sinks, b_w_out,
           c_norm, c_w_in, c_q_a_norm, c_w_qb, c_kv_a_norm, c_w_kvb, c_q_norm, c_k_norm, c_w_out):
    depth = a_norm.shape[0] + b_norm.shape[0] + c_norm.shape[0]
    xp, xs = x_prompt, x_sample
    outs = [[] for _ in range(12)]
    a_k_all = a_v_all = None
    for layer in range(depth):
        j, kind = divmod(layer, N_MIXERS)
        if kind == 0:
            res = _layer_a(xp, xs, cache_a_k, cache_a_v, j, a_k_all, a_v_all, a_norm[j], a_w_in[j],
                           a_q_norm[j], a_k_norm[j], a_rel_bias[j], a_w_out[j])
            a_k_all, a_v_all = res[4], res[5]
        elif kind == 1:
            res = _layer_b(xp, xs, cache_b_k[j], cache_b_v[j], b_norm[j], b_w_in[j], b_q_norm[j],
                           b_k_norm[j], b_sinks[j], b_w_out[j])
        else:
            res = _layer_c(xp, xs, cache_c_kv, cache_c_kr, j, c_norm[j], c_w_in[j], c_q_a_norm[j],
                           c_w_qb[j], c_kv_a_norm[j], c_w_kvb[j], c_q_norm[j], c_k_norm[j],
                           c_w_out[j])
        xp, xs = res[0], res[1]
        for slot in range(4):
            outs[4 * kind + slot].append(res[2 + slot])
    outs[2], outs[3] = None, None
    stacked = [None if o is None else jnp.stack(o) for o in outs]
    stacked[2], stacked[3] = a_k_all, a_v_all
    order = [0, 1, 4, 5, 8, 9, 2, 3, 6, 7, 10, 11]
    return (xp, xs) + tuple(stacked[i] for i in order)
```

```python
import functools
import math

import jax
import jax.numpy as jnp
from jax import lax
from jax.experimental import pallas as pl
from jax.experimental.pallas import tpu as pltpu

F32 = jnp.float32
BF16 = jnp.bfloat16

LANES = 128
CHUNK = 64
NORM_EPS = 1e-6
NEG_INF = -1e30
LOG2E = math.log2(math.e)
N_MIXERS = 3

A_HEADS = 8
A_HEAD_DIM = 128
A_BAND_CHUNKS = 9
A_REL_CLIP = 128
A_REACH = (A_BAND_CHUNKS - 1) * CHUNK

B_HEADS = 16
B_KV_HEADS = 4
B_HEAD_DIM = 64
B_BAND_CHUNKS = 3
B_REACH = (B_BAND_CHUNKS - 1) * CHUNK

C_HEADS = 16
C_NOPE = 64
C_ROPE = 32
C_QK = C_NOPE + C_ROPE
C_V = 64
C_Q_LORA = 512
C_KV_LORA = 256
ROPE_THETA = 10000.0

ROW_TILE = 512
OUT_ROW_TILE = 1024
ATTN_Q_BLOCK = 256
VMEM_LIMIT = 56 * 1024 * 1024
C_ATTN_VMEM_LIMIT = 60 * 1024 * 1024


def _params(n_axes, vmem_limit=VMEM_LIMIT):
    return pltpu.CompilerParams(dimension_semantics=("arbitrary",) * n_axes,
                                vmem_limit_bytes=vmem_limit)


def _const_spec(shape):
    zeros = (0,) * len(shape)
    return pl.BlockSpec(shape, lambda *_: zeros)


def _rms(x, width):
    return x * lax.rsqrt(jnp.sum(x * x, axis=-1, keepdims=True) * (1.0 / width) + NORM_EPS)


def _normed_input(x_ref, gn_ref):
    x = x_ref[0]
    return (_rms(x, x.shape[-1]) * gn_ref[...]).astype(BF16)


def _tile(val, t):
    return val[:, t * LANES:(t + 1) * LANES]


def _silu(g):
    return g / (1.0 + jnp.exp(-g))


def _qk(q, k):
    return lax.dot_general(q, k, (((1,), (1,)), ((), ())), preferred_element_type=F32)


def _proj_a_kernel(x_ref, gn_ref, w_ref, qg_ref, kg_ref,
                   q_ref, k_ref, v_ref, g_ref, kf_ref, vf_ref, *, scale, tail):
    xn = _normed_input(x_ref, gn_ref)
    width = A_HEADS * A_HEAD_DIM
    rows = xn.shape[0]
    zq = jnp.dot(xn, w_ref[:, 0:width], preferred_element_type=F32)
    for h in range(A_HEADS):
        qn = _rms(_tile(zq, h), A_HEAD_DIM) * qg_ref[...]
        q_ref[0, :, h * LANES:(h + 1) * LANES] = (qn * scale).astype(BF16)
    zk = jnp.dot(xn, w_ref[:, width:2 * width], preferred_element_type=F32)
    for h in range(A_HEADS):
        kn = _rms(_tile(zk, h), A_HEAD_DIM) * kg_ref[...]
        k_ref[0, :, h * LANES:(h + 1) * LANES] = kn.astype(BF16)
        kf_ref[0, :, h * LANES:(h + 1) * LANES] = kn[rows - tail:, :]
    zv = jnp.dot(xn, w_ref[:, 2 * width:3 * width], preferred_element_type=F32)
    v_ref[0] = zv.astype(BF16)
    vf_ref[0] = zv[rows - tail:, :]
    g_ref[0] = jnp.dot(xn, w_ref[:, 3 * width:4 * width], preferred_element_type=F32)


def _half_rms(z, lo, width):
    sq = z * z
    s_lo = jnp.sum(jnp.where(lo, sq, 0.0), axis=-1, keepdims=True)
    s_hi = jnp.sum(jnp.where(lo, 0.0, sq), axis=-1, keepdims=True)
    r = jnp.where(lo, lax.rsqrt(s_lo * (1.0 / width) + NORM_EPS),
                  lax.rsqrt(s_hi * (1.0 / width) + NORM_EPS))
    return z * r


def _expand_halves(t, lo, out_ref, u):
    rolled = pltpu.roll(t, B_HEAD_DIM, 1)
    zero = jnp.zeros_like(t)
    tiles = (jnp.where(lo, t, zero), jnp.where(lo, zero, rolled),
             jnp.where(lo, rolled, zero), jnp.where(lo, zero, t))
    for i, val in enumerate(tiles):
        c = 4 * u + i
        out_ref[0, :, c * LANES:(c + 1) * LANES] = val.astype(BF16)


def _proj_b_kernel(x_ref, gn_ref, w_ref, qg_ref, kg_ref,
                   q_ref, k_ref, v_ref, g_ref, kf_ref, vf_ref, *, scale, tail):
    xn = _normed_input(x_ref, gn_ref)
    rows = xn.shape[0]
    qw = B_HEADS * B_HEAD_DIM
    kw = B_KV_HEADS * B_HEAD_DIM
    lo = lax.broadcasted_iota(jnp.int32, (rows, LANES), 1) < B_HEAD_DIM
    zq = jnp.dot(xn, w_ref[:, 0:qw], preferred_element_type=F32)
    for t in range(qw // LANES):
        qn = _half_rms(_tile(zq, t), lo, B_HEAD_DIM) * qg_ref[...]
        q_ref[0, :, t * LANES:(t + 1) * LANES] = (qn * scale).astype(BF16)
    zk = jnp.dot(xn, w_ref[:, qw:qw + kw], preferred_element_type=F32)
    for u in range(kw // LANES):
        kn = _half_rms(_tile(zk, u), lo, B_HEAD_DIM) * kg_ref[...]
        kf_ref[0, :, u * LANES:(u + 1) * LANES] = kn[rows - tail:, :]
        _expand_halves(kn, lo, k_ref, u)
    zv = jnp.dot(xn, w_ref[:, qw + kw:qw + 2 * kw], preferred_element_type=F32)
    vf_ref[0] = zv[rows - tail:, :]
    for u in range(kw // LANES):
        _expand_halves(_tile(zv, u), lo, v_ref, u)
    g_ref[0] = jnp.dot(xn, w_ref[:, qw + 2 * kw:2 * qw + 2 * kw], preferred_element_type=F32)


def _rotate(x, cos_ref, sin_ref):
    return x * cos_ref[...] + pltpu.roll(x, LANES - C_ROPE // 2, 1) * sin_ref[...]


def _proj_c_kernel(x_ref, gn_ref, w_ref, qag_ref, wqb_ref, kvag_ref, qg_ref, krg_ref, group_ref,
                   cos_ref, sin_ref, q_ref, ckv_ref, kr_ref, g_ref, *, scale):
    xn = _normed_input(x_ref, gn_ref)
    rows = xn.shape[0]
    o1 = C_Q_LORA
    o2 = o1 + C_KV_LORA
    o3 = o2 + C_HEADS * C_V
    lane = lax.broadcasted_iota(jnp.int32, (rows, LANES), 1)
    nope = lane < C_NOPE
    rope = (lane >= C_NOPE) & (lane < C_QK)
    za = jnp.dot(xn, w_ref[:, 0:o1], preferred_element_type=F32)
    qa = (_rms(za, C_Q_LORA) * qag_ref[...]).astype(BF16)
    zq = jnp.dot(qa, wqb_ref[...], preferred_element_type=F32)
    for h in range(C_HEADS):
        z = _tile(zq, h)
        sums = jnp.dot((z * z).astype(BF16), group_ref[...], preferred_element_type=F32)
        r = lax.rsqrt(sums * jnp.where(nope, 1.0 / C_NOPE, 1.0 / C_ROPE) + NORM_EPS)
        qn = _rotate(z * r * qg_ref[...], cos_ref, sin_ref)
        q_ref[0, :, h * LANES:(h + 1) * LANES] = (qn * scale).astype(BF16)
    zc = jnp.dot(xn, w_ref[:, o1:o2], preferred_element_type=F32)
    ckv_ref[0] = _rms(zc, C_KV_LORA) * kvag_ref[...]
    g_ref[0] = jnp.dot(xn, w_ref[:, o2:o3], preferred_element_type=F32)
    zr = jnp.dot(xn, w_ref[:, o3:o3 + LANES], preferred_element_type=F32)
    s_k = jnp.sum(jnp.where(rope, zr * zr, 0.0), axis=-1, keepdims=True)
    krn = zr * lax.rsqrt(s_k * (1.0 / C_ROPE) + NORM_EPS) * krg_ref[...]
    kr_ref[0] = _rotate(krn, cos_ref, sin_ref)


def _expand_rows(ckv, kr, w_ref, kg_ref, k_scr, v_scr, rows_at):
    ckv = ckv.astype(BF16)
    width = C_HEADS * LANES
    lane = lax.broadcasted_iota(jnp.int32, (ckv.shape[0], LANES), 1)
    zk = jnp.dot(ckv, w_ref[:, 0:width], preferred_element_type=F32)
    for h in range(C_HEADS):
        kn = _rms(_tile(zk, h), C_NOPE) * kg_ref[...] + kr
        k_scr[rows_at, h * LANES:(h + 1) * LANES] = kn.astype(BF16)
    zv = jnp.dot(ckv, w_ref[:, width:2 * width], preferred_element_type=F32)
    for h in range(C_HEADS):
        ones_lane = C_V if h % 2 == 0 else 0
        v_scr[rows_at, h * LANES:(h + 1) * LANES] = jnp.where(lane == ones_lane, 1.0,
                                                              _tile(zv, h)).astype(BF16)


def _out_proj_kernel(x_ref, h_ref, w_ref, y_ref):
    y_ref[0] = x_ref[0] + jnp.dot(h_ref[0], w_ref[...], preferred_element_type=F32)


def _row_reduce(pieces, combine, reduce):
    tiles, narrow = [], []
    for piece in pieces:
        width = piece.shape[1]
        if width % LANES == 0:
            tiles += [piece[:, i * LANES:(i + 1) * LANES] for i in range(width // LANES)]
        else:
            narrow.append(piece)
    folded = ([functools.reduce(combine, tiles)] if tiles else []) + narrow
    return functools.reduce(combine, [reduce(x, axis=-1, keepdims=True) for x in folded])


def _softmax_parts(scores, sink):
    m = _row_reduce(scores, jnp.maximum, jnp.max)
    if sink is not None:
        m = jnp.maximum(m, sink)
    probs = [jnp.exp2(s - m) for s in scores]
    denom = _row_reduce(probs, jnp.add, jnp.sum)
    if sink is not None:
        denom = denom + jnp.exp2(sink - m)
    return probs, denom


def _window_attn_kernel(*refs, n_pieces, heads, need, has_bias, has_sinks):
    q_ref = refs[0]
    k_refs = refs[1:1 + n_pieces]
    v_refs = refs[1 + n_pieces:1 + 2 * n_pieces]
    pos = 1 + 2 * n_pieces
    g_ref = refs[pos]
    pos += 1
    bias_ref = None
    if has_bias:
        bias_ref = refs[pos]
        pos += 1
    sink_ref = None
    if has_sinks:
        sink_ref = refs[pos]
        pos += 1
    h_ref = refs[pos]

    step = pl.program_id(1)
    rows = q_ref.shape[1]
    lo = lax.broadcasted_iota(jnp.int32, (rows, LANES), 1) < LANES // 2
    pen = [jnp.where(step >= nd, 0.0, NEG_INF).astype(F32) if nd > 0 else None for nd in need]
    for t in range(len(heads)):
        acc = None
        for sub, (q_t, kv_t) in enumerate(heads[t]):
            head = t * len(heads[t]) + sub
            qt = q_ref[0, :, q_t * LANES:(q_t + 1) * LANES]
            scores = []
            off = 0
            for j in range(n_pieces):
                kj = k_refs[j][0, :, kv_t * LANES:(kv_t + 1) * LANES]
                s = _qk(qt, kj)
                if has_bias:
                    s = s + bias_ref[head, :, off:off + kj.shape[0]]
                if pen[j] is not None:
                    s = s + pen[j]
                scores.append(s)
                off += kj.shape[0]
            probs, denom = _softmax_parts(scores, sink_ref[head] if has_sinks else None)
            o = None
            for j in range(n_pieces):
                vj = v_refs[j][0, :, kv_t * LANES:(kv_t + 1) * LANES]
                pv = jnp.dot(probs[j].astype(BF16), vj, preferred_element_type=F32)
                o = pv if o is None else o + pv
            o = o / denom
            acc = o if acc is None else jnp.where(lo, acc, o)
        g = g_ref[0, :, t * LANES:(t + 1) * LANES]
        h_ref[0, :, t * LANES:(t + 1) * LANES] = (acc * _silu(g)).astype(BF16)


def _a_sample_kernel(q_ref, kn_ref, vn_ref, knf_ref, vnf_ref, ck_ref, cv_ref, g_ref, bias_ref,
                     *rest, slot):
    h_ref, ok_ref, ov_ref = rest[-3:]
    past = ck_ref.shape[2]
    new = q_ref.shape[1]
    reach = ok_ref.shape[2]
    drop = past + new - reach
    for other in range(ok_ref.shape[0]):
        if other != slot:
            ok_ref[other] = jnp.zeros(ok_ref.shape[1:], F32)
            ov_ref[other] = jnp.zeros(ov_ref.shape[1:], F32)
    k_heads = jnp.swapaxes(ck_ref[0, 0], 0, 1)
    v_heads = jnp.swapaxes(cv_ref[0, 0], 0, 1)
    k_out, v_out = [], []
    for h in range(A_HEADS):
        cols = slice(h * LANES, (h + 1) * LANES)
        kc = k_heads[h]
        vc = v_heads[h]
        qh = q_ref[0, :, cols]
        s_c = _qk(qh, kc.astype(BF16)) + bias_ref[h, :, 0:past]
        s_n = _qk(qh, kn_ref[0, :, cols]) + bias_ref[h, :, past:past + new]
        (p_c, p_n), denom = _softmax_parts([s_c, s_n], None)
        o = (jnp.dot(p_c.astype(BF16), vc.astype(BF16), preferred_element_type=F32)
             + jnp.dot(p_n.astype(BF16), vn_ref[0, :, cols], preferred_element_type=F32)) / denom
        h_ref[0, :, cols] = (o * _silu(g_ref[0, :, cols])).astype(BF16)
        k_out.append(jnp.concatenate([kc[drop:, :], knf_ref[0, :, cols]], axis=0))
        v_out.append(jnp.concatenate([vc[drop:, :], vnf_ref[0, :, cols]], axis=0))
    ok_ref[slot, 0] = jnp.swapaxes(jnp.stack(k_out), 0, 1)
    ov_ref[slot, 0] = jnp.swapaxes(jnp.stack(v_out), 0, 1)


def _online_update(c, q, kj, vj, bias, m_scr, acc_scr):
    s = _qk(q, kj)
    if bias is not None:
        s = s + bias
    parts = [s[:, i * LANES:(i + 1) * LANES] for i in range(s.shape[1] // LANES)]
    m_old = m_scr[c]
    m_new = jnp.maximum(m_old, jnp.max(functools.reduce(jnp.maximum, parts), axis=-1,
                                       keepdims=True))
    m_scr[c] = m_new
    p = jnp.concatenate([jnp.exp2(x - m_new).astype(BF16) for x in parts], axis=1)
    acc_scr[c] = (acc_scr[c] * jnp.exp2(m_old - m_new)
                  + jnp.dot(p, vj, preferred_element_type=F32))


def _band_attn_kernel(*refs, n_pieces, heads, need, has_sinks):
    q_ref = refs[0]
    k_refs = refs[1:1 + n_pieces]
    v_refs = refs[1 + n_pieces:1 + 2 * n_pieces]
    g_ref, bias_ref = refs[1 + 2 * n_pieces:3 + 2 * n_pieces]
    pos = 3 + 2 * n_pieces
    sink_ref = None
    if has_sinks:
        sink_ref = refs[pos]
        pos += 1
    x_ref, wo_ref, y_ref, s_scr, m_scr, h_scr = refs[pos:pos + 6]

    step = pl.program_id(1)
    rows = q_ref.shape[1]
    lo = lax.broadcasted_iota(jnp.int32, (rows, LANES), 1) < LANES // 2
    flat = [(t, sub, q_t, kv_t) for t in range(len(heads)) for sub, (q_t, kv_t) in enumerate(heads[t])]
    per_tile = len(heads[0])

    offsets = [sum(k_refs[i].shape[1] for i in range(j)) for j in range(n_pieces)]

    for j in range(n_pieces):
        width = k_refs[j].shape[1]
        pen = jnp.where(step >= need[j], 0.0, NEG_INF).astype(F32) if need[j] > 0 else None
        for t, sub, q_t, kv_t in flat:
            c = t * per_tile + sub
            s = _qk(q_ref[0, :, q_t * LANES:(q_t + 1) * LANES],
                    k_refs[j][0, :, kv_t * LANES:(kv_t + 1) * LANES])
            s = s + bias_ref[c, :, offsets[j]:offsets[j] + width]
            if pen is not None:
                s = s + pen
            s_scr[c, :, offsets[j]:offsets[j] + width] = s
            tiles = [s[:, i * LANES:(i + 1) * LANES] for i in range(width // LANES)]
            if j > 0:
                tiles.append(m_scr[c])
            m_scr[c] = functools.reduce(jnp.maximum, tiles)

    slot0 = jnp.minimum(step, 0)
    for t in range(len(heads)):
        out = None
        for sub, (_, kv_t) in enumerate(heads[t]):
            c = t * per_tile + sub
            m = jnp.max(m_scr[c + slot0], axis=-1, keepdims=True)
            if has_sinks:
                m = jnp.maximum(m, sink_ref[c])
            lanes_sum = None
            o = None
            for j in range(n_pieces):
                width = k_refs[j].shape[1]
                p = jnp.exp2(s_scr[c + slot0, :, offsets[j]:offsets[j] + width] - m)
                for i in range(width // LANES):
                    tile = p[:, i * LANES:(i + 1) * LANES]
                    lanes_sum = tile if lanes_sum is None else lanes_sum + tile
                pv = jnp.dot(p.astype(BF16), v_refs[j][0, :, kv_t * LANES:(kv_t + 1) * LANES],
                             preferred_element_type=F32)
                o = pv if o is None else o + pv
            denom = jnp.sum(lanes_sum, axis=-1, keepdims=True)
            if has_sinks:
                denom = denom + jnp.exp2(sink_ref[c] - m)
            o = o / denom
            out = o if out is None else jnp.where(lo, out, o)
        g = g_ref[0, :, t * LANES:(t + 1) * LANES]
        h_scr[:, t * LANES:(t + 1) * LANES] = (out * _silu(g)).astype(BF16)

    y_ref[0] = x_ref[0] + jnp.dot(h_scr[...], wo_ref[...], preferred_element_type=F32)


def _c_attn_kernel(q_ref, ckv_ref, kr_ref, w_ref, kg_ref, g_ref, mask_ref, x_ref, wo_ref, y_ref,
                   k_scr, v_scr, m_scr, acc_scr, h_scr, *, kb, expand_tile):
    step = pl.program_id(1)
    rows = q_ref.shape[1]
    lane = lax.broadcasted_iota(jnp.int32, (rows, LANES), 1)
    lo = lane < C_V
    per_q = rows // kb

    @pl.when(step == 0)
    def _expand_sequence():
        def expand(r, carry):
            at = pl.ds(pl.multiple_of(r * expand_tile, expand_tile), expand_tile)
            _expand_rows(ckv_ref[0, at, :], kr_ref[0, at, :], w_ref, kg_ref, k_scr, v_scr, at)
            return carry
        lax.fori_loop(0, ckv_ref.shape[1] // expand_tile, expand, 0)

    m_scr[...] = jnp.full(m_scr.shape, NEG_INF, F32)
    acc_scr[...] = jnp.zeros(acc_scr.shape, F32)

    def block(j, masked):
        start = pl.multiple_of(j * kb, kb)
        for c in range(C_HEADS):
            cols = slice(c * LANES, (c + 1) * LANES)
            _online_update(c, q_ref[0, :, cols], k_scr[pl.ds(start, kb), cols],
                           v_scr[pl.ds(start, kb), cols],
                           None if masked is None else mask_ref[masked], m_scr, acc_scr)

    def body(j, carry):
        block(j, None)
        return carry

    lax.fori_loop(0, step * per_q, body, 0)
    for d in range(per_q):
        block(step * per_q + d, d)

    for t in range(C_HEADS // 2):
        a_e = acc_scr[2 * t]
        a_o = acc_scr[2 * t + 1]
        l_e = jnp.sum(jnp.where(lane == C_V, a_e, 0.0), axis=-1, keepdims=True)
        l_o = jnp.sum(jnp.where(lane == 0, a_o, 0.0), axis=-1, keepdims=True)
        g = g_ref[0, :, t * LANES:(t + 1) * LANES]
        o = jnp.where(lo, a_e / l_e, a_o / l_o)
        h_scr[:, t * LANES:(t + 1) * LANES] = (o * _silu(g)).astype(BF16)

    y_ref[0] = x_ref[0] + jnp.dot(h_scr[...], wo_ref[...], preferred_element_type=F32)


def _c_sample_kernel(q_ref, ckv_ref, ckr_ref, nkv_ref, nkr_ref, w_ref, kg_ref, place_ref, g_ref,
                     h_ref, k_scr, v_scr, *, expand_tile):
    past = ckv_ref.shape[1]
    new = q_ref.shape[1]
    lo = lax.broadcasted_iota(jnp.int32, (new, LANES), 1) < C_V

    def expand(r, carry):
        at = pl.ds(pl.multiple_of(r * expand_tile, expand_tile), expand_tile)
        kr = jnp.dot(ckr_ref[0, at, :].astype(BF16), place_ref[...], preferred_element_type=F32)
        _expand_rows(ckv_ref[0, at, :], kr, w_ref, kg_ref, k_scr, v_scr, at)
        return carry
    lax.fori_loop(0, past // expand_tile, expand, 0)
    _expand_rows(nkv_ref[0], nkr_ref[0], w_ref, kg_ref, k_scr, v_scr, pl.ds(past, new))

    for t in range(C_HEADS // 2):
        acc = None
        for c in (2 * t, 2 * t + 1):
            cols = slice(c * LANES, (c + 1) * LANES)
            q = q_ref[0, :, cols]
            s_c = _qk(q, k_scr[0:past, cols])
            s_n = _qk(q, k_scr[past:past + new, cols])
            (p_c, p_n), denom = _softmax_parts([s_c, s_n], None)
            o = (jnp.dot(p_c.astype(BF16), v_scr[0:past, cols], preferred_element_type=F32)
                 + jnp.dot(p_n.astype(BF16), v_scr[past:past + new, cols],
                           preferred_element_type=F32)) / denom
            acc = o if acc is None else jnp.where(lo, acc, o)
        g = g_ref[0, :, t * LANES:(t + 1) * LANES]
        h_ref[0, :, t * LANES:(t + 1) * LANES] = (acc * _silu(g)).astype(BF16)


def _rows(tm, width):
    return pl.BlockSpec((1, tm, width), lambda b, i: (b, i, 0))


def _cache_out(n, s, tm, width, tail):
    if tail is None:
        return jax.ShapeDtypeStruct((n, s, width), F32), _rows(tm, width), tm
    assert tail <= tm and s % tm == 0
    spec = pl.BlockSpec((1, tail, width), lambda b, i: (b, 0, 0))
    return jax.ShapeDtypeStruct((n, tail, width), F32), spec, tail


def _row_tile(s):
    tm = min(ROW_TILE, s)
    assert s % tm == 0
    return tm


def _proj_ab(kernel, x, gn, w, qg, kg, *, q_width, kv_width, g_width, scale, tail, name):
    n, s, d = x.shape
    tm = _row_tile(s)
    cache_width = (w.shape[1] - q_width - g_width) // 2
    cache_shape, cache_spec, tail_rows = _cache_out(n, s, tm, cache_width, tail)
    return pl.pallas_call(
        functools.partial(kernel, scale=scale, tail=tail_rows),
        grid=(n, s // tm),
        in_specs=[_rows(tm, d), _const_spec(gn.shape), _const_spec(w.shape),
                  _const_spec(qg.shape), _const_spec(kg.shape)],
        out_specs=[_rows(tm, q_width), _rows(tm, kv_width), _rows(tm, kv_width), _rows(tm, g_width),
                   cache_spec, cache_spec],
        out_shape=[jax.ShapeDtypeStruct((n, s, q_width), BF16),
                   jax.ShapeDtypeStruct((n, s, kv_width), BF16),
                   jax.ShapeDtypeStruct((n, s, kv_width), BF16),
                   jax.ShapeDtypeStruct((n, s, g_width), F32), cache_shape, cache_shape],
        compiler_params=_params(2), name=name)(x, gn, w, qg, kg)


def _out_proj(x, h, w):
    n, s, d = x.shape
    tm = math.gcd(s, OUT_ROW_TILE)
    return pl.pallas_call(
        _out_proj_kernel, grid=(n, s // tm),
        in_specs=[_rows(tm, d), _rows(tm, h.shape[-1]), _const_spec(w.shape)],
        out_specs=_rows(tm, d), out_shape=jax.ShapeDtypeStruct((n, s, d), F32),
        compiler_params=_params(2), name="out_proj")(x, h, w)


def _window_attn(q, k_pieces, v_pieces, g, bias, sinks, *, grid, q_map, q_block, heads, need, name):
    width = len(heads) * LANES
    in_specs = [pl.BlockSpec((1, q_block, q.shape[-1]), q_map)]
    args = [q]
    for arr, rows, imap in list(k_pieces) + list(v_pieces):
        in_specs.append(pl.BlockSpec((1, rows, arr.shape[-1]), imap))
        args.append(arr)
    in_specs.append(pl.BlockSpec((1, q_block, width), q_map))
    args.append(g)
    if bias is not None:
        in_specs.append(_const_spec(bias.shape))
        args.append(bias)
    if sinks is not None:
        in_specs.append(pl.BlockSpec(memory_space=pltpu.SMEM))
        args.append(sinks)
    kernel = functools.partial(_window_attn_kernel, n_pieces=len(k_pieces), heads=heads, need=need,
                               has_bias=bias is not None, has_sinks=sinks is not None)
    return pl.pallas_call(
        kernel, grid=grid, in_specs=in_specs,
        out_specs=pl.BlockSpec((1, q_block, width), q_map),
        out_shape=jax.ShapeDtypeStruct(g.shape[:2] + (width,), BF16),
        compiler_params=_params(2), name=name)(*args)


def _band_bias(base, q_rows, k_rows, q_off, band_chunks):
    m = q_rows + k_rows - 1
    vec = base(q_off + (q_rows - 1) - jnp.arange(m)) * LOG2E
    flat = jnp.tile(vec, (1, q_rows))[:, q_rows - 1:q_rows - 1 + q_rows * (m - 1)]
    bias = flat.reshape(-1, q_rows, m - 1)[:, :, :k_rows]
    if band_chunks is None:
        return bias
    i = jnp.arange(q_rows)[:, None]
    j = jnp.arange(k_rows)[None, :]
    lag = (q_off + i) // CHUNK - j // CHUNK
    return jnp.where((lag >= 0) & (lag < band_chunks), bias, NEG_INF)


def _band_prompt_pieces(arr, q_block, reach):
    pieces, need = [], []
    if reach % q_block == 0:
        for r in range(reach // q_block, 0, -1):
            pieces.append((arr, q_block, lambda b, i, r=r: (b, jnp.maximum(i - r, 0), 0)))
            need.append(r)
    else:
        assert q_block % reach == 0
        ratio = q_block // reach
        pieces.append((arr, reach, lambda b, i: (b, jnp.maximum(i * ratio - 1, 0), 0)))
        need.append(1)
    pieces.append((arr, q_block, lambda b, i: (b, i, 0)))
    need.append(0)
    return pieces, tuple(need)


def _band_prompt(proj, xp, wo, *, reach, band_chunks, heads, bias_base, sinks, name):
    n, s, d = xp.shape
    qb = min(ATTN_Q_BLOCK, s)
    assert s % qb == 0 and s >= reach
    q, k, v, g, kf, vf = proj(xp, reach)
    kp, need = _band_prompt_pieces(k, qb, reach)
    vp, _ = _band_prompt_pieces(v, qb, reach)
    bias = _band_bias(bias_base, qb, reach + qb, reach, band_chunks)
    width = len(heads) * LANES
    n_slots = sum(len(hs) for hs in heads)
    pieces = kp + vp
    args = [q] + [arr for arr, _, _ in pieces] + [g, bias]
    in_specs = ([_rows(qb, q.shape[-1])]
                + [pl.BlockSpec((1, rows, arr.shape[-1]), imap) for arr, rows, imap in pieces]
                + [_rows(qb, width), _single(bias.shape, lambda b, i: (0, 0, 0))])
    if sinks is not None:
        in_specs.append(pl.BlockSpec(memory_space=pltpu.SMEM))
        args.append(sinks)
    in_specs += [_rows(qb, d), _single(wo.shape, lambda b, i: (0, 0))]
    args += [xp, wo]
    scratch = [pltpu.VMEM((n_slots, qb, reach + qb), F32), pltpu.VMEM((n_slots, qb, LANES), F32),
               pltpu.VMEM((qb, width), BF16)]
    yp = pl.pallas_call(
        functools.partial(_band_attn_kernel, n_pieces=len(kp), heads=heads, need=need,
                          has_sinks=sinks is not None),
        grid=(n, s // qb), in_specs=in_specs, out_specs=_rows(qb, d),
        out_shape=jax.ShapeDtypeStruct((n, s, d), F32),
        scratch_shapes=scratch,
        compiler_params=_params(2), name=name + "_attn_prompt")(*args)
    return yp, kf, vf


def _layer_a(xp, xs, cache_k, cache_v, layer, prev_k, prev_v, norm_g, w_in, q_g, k_g, rel, w_out):
    width = A_HEADS * A_HEAD_DIM
    w = w_in.astype(BF16)
    wo = w_out.astype(BF16)
    gn, qg, kg = norm_g.reshape(1, -1), q_g.reshape(1, -1), k_g.reshape(1, -1)

    head_shape = (A_HEADS, A_HEAD_DIM)

    def proj(x, tail):
        return _proj_ab(_proj_a_kernel, x, gn, w, qg, kg, q_width=width, kv_width=width,
                        g_width=width, scale=A_HEAD_DIM ** -0.5 * LOG2E, tail=tail, name="proj_a")

    def bias_base(dist):
        idx = jnp.clip(dist, -A_REL_CLIP, A_REL_CLIP) + A_REL_CLIP
        return rel.astype(F32)[:, idx]

    heads = tuple(((t, t),) for t in range(A_HEADS))
    yp, kf, vf = _band_prompt(proj, xp, wo, reach=A_REACH, band_chunks=A_BAND_CHUNKS, heads=heads,
                              bias_base=bias_base, sinks=None, name="a")
    n = xp.shape[0]
    kf = kf.reshape((n, A_REACH) + head_shape)
    vf = vf.reshape((n, A_REACH) + head_shape)

    n_layers, ns, past = cache_k.shape[:3]
    ts, d = xs.shape[1:]
    xs2 = xs.reshape(1, ns * ts, d)
    q, k, v, g, knf, vnf = proj(xs2, None)
    bias_s = _band_bias(bias_base, ts, past + ts, past, None)
    new = lambda w_: pl.BlockSpec((1, ts, w_), lambda b, i: (0, i, 0))
    cache_block = lambda rows: pl.BlockSpec((1, 1, rows) + head_shape,
                                            lambda b, i: (layer, i, 0, 0, 0))
    out_cache = jax.ShapeDtypeStruct((n_layers, ns, A_REACH) + head_shape, F32)
    prev = [] if prev_k is None else [prev_k, prev_v]
    if prev:
        out_block, slot = cache_block(A_REACH), 0
    else:
        out_block = pl.BlockSpec((n_layers, 1, A_REACH) + head_shape, lambda b, i: (0, i, 0, 0, 0))
        slot = layer
    n_in = 9
    h, k_all, v_all = pl.pallas_call(
        functools.partial(_a_sample_kernel, slot=slot), grid=(1, ns),
        in_specs=[new(width)] * 5 + [cache_block(past)] * 2 + [new(width), _const_spec(bias_s.shape)]
        + [pl.BlockSpec(memory_space=pl.ANY)] * len(prev),
        out_specs=[new(width), out_block, out_block],
        out_shape=[jax.ShapeDtypeStruct((1, ns * ts, width), BF16), out_cache, out_cache],
        input_output_aliases={n_in + i: 1 + i for i in range(len(prev))},
        compiler_params=_params(2), name="a_attn_sample")(
            q, k, v, knf, vnf, cache_k, cache_v, g, bias_s, *prev)
    ys = _out_proj(xs2, h, wo).reshape(xs.shape)
    return yp, ys, kf, vf, k_all, v_all


def _layer_b(xp, xs, ck, cv, norm_g, w_in, q_g, k_g, sinks, w_out):
    qw = B_HEADS * B_HEAD_DIM
    kw = B_KV_HEADS * B_HEAD_DIM
    w = w_in.astype(BF16)
    wo = w_out.astype(BF16)
    gn = norm_g.reshape(1, -1)
    qg = jnp.tile(q_g.reshape(1, -1), (1, 2))
    kg = jnp.tile(k_g.reshape(1, -1), (1, 2))
    sinks2 = sinks.astype(F32) * LOG2E

    def proj(x, tail):
        return _proj_ab(_proj_b_kernel, x, gn, w, qg, kg, q_width=qw, kv_width=4 * kw, g_width=qw,
                        scale=B_HEAD_DIM ** -0.5 * LOG2E, tail=tail, name="proj_b")

    slopes = 2.0 ** (-8.0 * jnp.arange(1, B_HEADS + 1, dtype=F32) / B_HEADS)

    def bias_base(dist):
        return -slopes[:, None] * jnp.abs(dist).astype(F32)[None, :]

    def expand_cache(c):
        n, rows, _ = c.shape
        c4 = c.reshape(n, rows, B_KV_HEADS, 1, B_HEAD_DIM)
        z = jnp.zeros_like(c4)
        lo = jnp.concatenate([c4, z], axis=-1)
        hi = jnp.concatenate([z, c4], axis=-1)
        return jnp.concatenate([lo, hi], axis=3).reshape(n, rows, 4 * kw).astype(BF16)

    heads = tuple(((t, 2 * (t // 2)), (t, 2 * (t // 2) + 1)) for t in range(qw // LANES))
    yp, kf, vf = _band_prompt(proj, xp, wo, reach=B_REACH, band_chunks=B_BAND_CHUNKS, heads=heads,
                              bias_base=bias_base, sinks=sinks2, name="b")
    n = xp.shape[0]
    head_shape = ck.shape[2:]

    ns, ts, d = xs.shape
    past = ck.shape[1]
    xs2 = xs.reshape(1, ns * ts, d)
    q, k, v, g, kf_s, vf_s = proj(xs2, None)
    ckf = ck.reshape(ns, past, -1)
    cvf = cv.reshape(ns, past, -1)
    cache_map = lambda b, i: (i, 0, 0)
    new_map = lambda b, i: (0, i, 0)
    bias_s = _band_bias(bias_base, ts, past + ts, past, None)
    h = _window_attn(q, [(expand_cache(ckf), past, cache_map), (k, ts, new_map)],
                     [(expand_cache(cvf), past, cache_map), (v, ts, new_map)], g, bias_s, sinks2,
                     grid=(1, ns), q_map=new_map, q_block=ts, heads=heads, need=(0, 0),
                     name="b_attn_sample")
    ys = _out_proj(xs2, h, wo).reshape(xs.shape)
    k_all = jnp.concatenate([ckf, kf_s.reshape(ns, ts, -1)], axis=1)[:, -B_REACH:]
    v_all = jnp.concatenate([cvf, vf_s.reshape(ns, ts, -1)], axis=1)[:, -B_REACH:]
    return (yp, ys, kf.reshape((n, B_REACH) + head_shape), vf.reshape((n, B_REACH) + head_shape),
            k_all.reshape((ns, B_REACH) + head_shape), v_all.reshape((ns, B_REACH) + head_shape))


def _rope_tables(pos):
    half = C_ROPE // 2
    inv = ROPE_THETA ** (-jnp.arange(half, dtype=F32) / half)
    ang = pos.astype(F32)[:, None] * inv[None, :]
    cos, sin = jnp.cos(ang), jnp.sin(ang)
    rows = pos.shape[0]
    ones = jnp.ones((rows, C_NOPE), F32)
    zeros = lambda width: jnp.zeros((rows, width), F32)
    pad = LANES - C_QK
    cos_t = jnp.concatenate([ones, cos, cos, zeros(pad)], axis=1)
    sin_t = jnp.concatenate([zeros(C_NOPE), -sin, sin, zeros(pad)], axis=1)
    return cos_t, sin_t


def _rope_padded(x):
    half = C_ROPE // 2
    zeros = jnp.zeros(x.shape[:-1] + (LANES - C_QK - half,), x.dtype)
    return jnp.concatenate([x, x[..., C_NOPE:C_NOPE + half], zeros], axis=-1)


def _proj_c(x, pos, gn, w, qag, wqb, kvag, qg, krg):
    n, s, d = x.shape
    tm = _row_tile(s)
    per_seq = pos.shape[0] // tm
    tables = _rope_tables(pos)
    table_spec = pl.BlockSpec((tm, LANES), lambda b, i: (i % per_seq, 0))
    lane = jnp.arange(LANES)
    src = jnp.where(lane < C_NOPE, 0, jnp.where(lane < C_QK, 1, 2))
    dst = jnp.where(lane < C_NOPE, 0, 1)
    group = (src[:, None] == dst[None, :]).astype(BF16)
    consts = (gn, w, qag, wqb, kvag, qg, krg, group)
    return pl.pallas_call(
        functools.partial(_proj_c_kernel, scale=C_QK ** -0.5 * LOG2E),
        grid=(n, s // tm),
        in_specs=[_rows(tm, d)] + [_const_spec(c.shape) for c in consts] + [table_spec] * 2,
        out_specs=[_rows(tm, C_HEADS * LANES), _rows(tm, C_KV_LORA), _rows(tm, LANES),
                   _rows(tm, C_HEADS * C_V)],
        out_shape=[jax.ShapeDtypeStruct((n, s, C_HEADS * LANES), BF16),
                   jax.ShapeDtypeStruct((n, s, C_KV_LORA), F32),
                   jax.ShapeDtypeStruct((n, s, LANES), F32),
                   jax.ShapeDtypeStruct((n, s, C_HEADS * C_V), F32)],
        compiler_params=_params(2), name="proj_c")(x, *consts, *tables)


def _single(shape, index_map):
    return pl.BlockSpec(shape, index_map, pipeline_mode=pl.Buffered(1))


def _c_attn(q, ckv, kr, w, kg, g, x, wo):
    n, s, _ = q.shape
    qb = min(ATTN_Q_BLOCK, s)
    kb = qb
    assert s % qb == 0 and qb % kb == 0 and kb % CHUNK == 0
    per_q = qb // kb
    row_chunk = jnp.arange(qb)[None, :, None] // CHUNK
    col_chunk = (jnp.arange(per_q)[:, None, None] * kb + jnp.arange(kb)[None, None, :]) // CHUNK
    mask = jnp.where(col_chunk <= row_chunk, 0.0, NEG_INF).astype(F32)
    width = C_HEADS * LANES
    whole = lambda w_: _single((1, s, w_), lambda b, i: (b, 0, 0))
    const = lambda a: _single(a.shape, lambda b, i: (0,) * a.ndim)
    d = x.shape[-1]
    return pl.pallas_call(
        functools.partial(_c_attn_kernel, kb=kb, expand_tile=qb), grid=(n, s // qb),
        in_specs=[_rows(qb, width), whole(C_KV_LORA), whole(LANES), const(w), const(kg),
                  _rows(qb, g.shape[-1]), const(mask), _rows(qb, d), const(wo)],
        out_specs=_rows(qb, d),
        out_shape=jax.ShapeDtypeStruct(x.shape, F32),
        scratch_shapes=[pltpu.VMEM((s, width), BF16), pltpu.VMEM((s, width), BF16),
                        pltpu.VMEM((C_HEADS, qb, LANES), F32), pltpu.VMEM((C_HEADS, qb, LANES), F32),
                        pltpu.VMEM((qb, g.shape[-1]), BF16)],
        compiler_params=_params(2, C_ATTN_VMEM_LIMIT), name="c_attn_prompt")(
            q, ckv, kr, w, kg, g, mask, x, wo)


def _c_attn_sample(q, cache_kv, cache_kr, first, new_kv, new_kr, w, kg, g, ns):
    past = cache_kv.shape[1]
    ts = q.shape[1] // ns
    width = C_HEADS * LANES
    tile = math.gcd(past, ATTN_Q_BLOCK)
    place = jnp.pad(jnp.eye(C_ROPE, dtype=BF16), ((0, 0), (C_NOPE, LANES - C_QK)))
    new = lambda w_: pl.BlockSpec((1, ts, w_), lambda b, i: (0, i, 0))
    cached = lambda w_: pl.BlockSpec((1, past, w_), lambda b, i: (first + i, 0, 0))
    return pl.pallas_call(
        functools.partial(_c_sample_kernel, expand_tile=tile), grid=(1, ns),
        in_specs=[new(width), cached(C_KV_LORA), cached(C_ROPE), new(C_KV_LORA), new(LANES),
                  _const_spec(w.shape), _const_spec(kg.shape), _const_spec(place.shape),
                  new(g.shape[-1])],
        out_specs=new(g.shape[-1]),
        out_shape=jax.ShapeDtypeStruct(g.shape, BF16),
        scratch_shapes=[pltpu.VMEM((past + ts, width), BF16), pltpu.VMEM((past + ts, width), BF16)],
        compiler_params=_params(2), name="c_attn_sample")(
            q, cache_kv, cache_kr, new_kv, new_kr, w, kg, place, g)


def _layer_c(xp, xs, cache_kv, cache_kr, layer, norm_g, w_in, qa_g, w_qb, kva_g, w_kvb, q_g, k_g,
             w_out):
    n, s, d = xp.shape
    o1 = C_Q_LORA
    o2 = o1 + C_KV_LORA
    o3 = o2 + C_ROPE
    kr_cols = _rope_padded(jnp.pad(w_in[:, o2:o3], ((0, 0), (C_NOPE, 0))))
    w = jnp.concatenate([w_in[:, :o2], w_in[:, o3:], kr_cols], axis=1).astype(BF16)
    wqb = _rope_padded(w_qb.reshape(C_Q_LORA, C_HEADS, C_QK))
    wqb = wqb.reshape(C_Q_LORA, C_HEADS * LANES).astype(BF16)
    kvb = w_kvb.reshape(C_KV_LORA, C_HEADS, C_NOPE + C_V)
    wk = jnp.pad(kvb[..., :C_NOPE], ((0, 0), (0, 0), (0, LANES - C_NOPE)))
    wv = kvb[..., C_NOPE:].reshape(C_KV_LORA, C_HEADS // 2, 2, C_V)
    zv = jnp.zeros_like(wv[:, :, 0])
    wv = jnp.stack([jnp.concatenate([wv[:, :, 0], zv], axis=-1),
                    jnp.concatenate([zv, wv[:, :, 1]], axis=-1)], axis=2)
    wkv = jnp.concatenate([wk.reshape(C_KV_LORA, -1), wv.reshape(C_KV_LORA, -1)], axis=1).astype(BF16)
    wo = w_out.astype(BF16)

    gn, qag, kvag = norm_g.reshape(1, -1), qa_g.reshape(1, -1), kva_g.reshape(1, -1)
    qg = _rope_padded(q_g.reshape(1, -1))
    krg = _rope_padded(jnp.pad(k_g[C_NOPE:].reshape(1, -1), ((0, 0), (C_NOPE, 0))))
    kg = jnp.pad(k_g[:C_NOPE].reshape(1, -1), ((0, 0), (0, LANES - C_NOPE)))

    q, ckv_p, kr_p, g = _proj_c(xp, jnp.arange(s), gn, w, qag, wqb, kvag, qg, krg)
    yp = _c_attn(q, ckv_p, kr_p, wkv, kg, g, xp, wo)

    n_layers, ns, past = cache_kv.shape[:3]
    ts = xs.shape[1]
    xs2 = xs.reshape(1, ns * ts, d)
    pos = past + jnp.arange(ts)
    q, ckv_s, kr_s, g = _proj_c(xs2, jnp.tile(pos, ns), gn, w, qag, wqb, kvag, qg, krg)
    h = _c_attn_sample(q, cache_kv.reshape(n_layers * ns, past, C_KV_LORA),
                       cache_kr.reshape(n_layers * ns, past, C_ROPE), layer * ns,
                       ckv_s, kr_s, wkv, kg, g, ns)
    ys = _out_proj(xs2, h, wo).reshape(xs.shape)
    rope = slice(C_NOPE, C_QK)
    return (yp, ys, ckv_p, kr_p[..., rope], ckv_s.reshape(ns, ts, C_KV_LORA),
            kr_s.reshape(ns, ts, LANES)[..., rope])


def kernel(x_prompt, x_sample, cache_a_k, cache_a_v, cache_b_k, cache_b_v, cache_c_kv, cache_c_kr,
           a_norm, a_w_in, a_q_norm, a_k_norm, a_rel_bias, a_w_out,
           b_norm, b_w_in, b_q_norm, b_k_norm, b_sinks, b_w_out,
           c_norm, c_w_in, c_q_a_norm, c_w_qb, c_kv_a_norm, c_w_kvb, c_q_norm, c_k_norm, c_w_out):
    depth = a_norm.shape[0] + b_norm.shape[0] + c_norm.shape[0]
    xp, xs = x_prompt, x_sample
    outs = [[] for _ in range(12)]
    a_k_all = a_v_all = None
    for layer in range(depth):
        j, kind = divmod(layer, N_MIXERS)
        if kind == 0:
            res = _layer_a(xp, xs, cache_a_k, cache_a_v, j, a_k_all, a_v_all, a_norm[j], a_w_in[j],
                           a_q_norm[j], a_k_norm[j], a_rel_bias[j], a_w_out[j])
            a_k_all, a_v_all = res[4], res[5]
        elif kind == 1:
            res = _layer_b(xp, xs, cache_b_k[j], cache_b_v[j], b_norm[j], b_w_in[j], b_q_norm[j],
                           b_k_norm[j], b_sinks[j], b_w_out[j])
        else:
            res = _layer_c(xp, xs, cache_c_kv, cache_c_kr, j, c_norm[j], c_w_in[j], c_q_a_norm[j],
                           c_w_qb[j], c_kv_a_norm[j], c_w_kvb[j], c_q_norm[j], c_k_norm[j],
                           c_w_out[j])
        xp, xs = res[0], res[1]
        for slot in range(4):
            outs[4 * kind + slot].append(res[2 + slot])
    outs[2], outs[3] = None, None
    stacked = [None if o is None else jnp.stack(o) for o in outs]
    stacked[2], stacked[3] = a_k_all, a_v_all
    order = [0, 1, 4, 5, 8, 9, 2, 3, 6, 7, 10, 11]
    return (xp, xs) + tuple(stacked[i] for i in order)
```
